```python
import math, functools
import jax, jax.numpy as jnp
from jax import lax
import numpy as np

D_MODEL = 1024
BATCH = 8
SEQ = 2048
DEPTH = 2
DEC_BATCH = 32
DEC_SEQ = 8
PAST_LEN = 8192
PAGE_SIZE = 128

N_HEADS = 8
HEAD_DIM = 64
V_DIM = 2 * HEAD_DIM
QK_WIDTH = N_HEADS * 2 * HEAD_DIM
ATTN_WIDTH = N_HEADS * V_DIM
Q_BLOCK = 128
ROPE_THETA = 10000.0
SUBLN_EPS = 1e-5
D_RNN = D_MODEL
N_RG_BLOCKS = 16
RG_BLOCK = D_RNN // N_RG_BLOCKS
CONV_W = 4
RG_C = 8.0
D_FF = 4 * D_MODEL
NORM_EPS = 1e-6
IN_WIDTH = 2 * D_RNN + 2 * QK_WIDTH + ATTN_WIDTH + 2 * D_MODEL
SPLIT_POINTS = tuple(int(s) for s in np.cumsum([D_RNN, D_RNN, QK_WIDTH, QK_WIDTH, ATTN_WIDTH, D_MODEL]))

kernel_name = 'hybrid_rglru_diffattn_step'


def rmsnorm(x, g, eps=NORM_EPS):
    xf = x.astype(jnp.float32)
    y = xf * lax.rsqrt(jnp.mean(xf * xf, axis=-1, keepdims=True) + eps)
    return (y * g.astype(jnp.float32)).astype(x.dtype)


def rope(x, pos):
    half = HEAD_DIM // 2
    inv = ROPE_THETA ** (-2.0 * jnp.arange(half, dtype=jnp.float32) / HEAD_DIM)
    ang = pos.astype(jnp.float32)[:, None] * inv[None, :]
    cos = jnp.cos(ang)[None, :, None, :]
    sin = jnp.sin(ang)[None, :, None, :]
    xf = x.astype(jnp.float32)
    x1, x2 = xf[..., :half], xf[..., half:]
    return jnp.concatenate([x1 * cos - x2 * sin, x2 * cos + x1 * sin], axis=-1).astype(x.dtype)


def causal_conv(xr, buf, w, b):
    T = xr.shape[1]
    xp = jnp.concatenate([buf.astype(xr.dtype), xr], axis=1)
    out = b
    for j in range(CONV_W):
        out = out + xp[:, j:j + T] * w[j]
    return out, xp[:, -(CONV_W - 1):]


def block_diag(x, w, b):
    B, T, C = x.shape
    xb = x.reshape(B, T, N_RG_BLOCKS, RG_BLOCK)
    return jnp.einsum('btnc,ncd->btnd', xb, w).reshape(B, T, C) + b


def rglru(x, h0, wa, ba, wx, bx, lam):
    r = jax.nn.sigmoid(block_diag(x, wa, ba).astype(jnp.float32))
    i = jax.nn.sigmoid(block_diag(x, wx, bx).astype(jnp.float32))
    log_a = -RG_C * r * jax.nn.softplus(-lam.astype(jnp.float32))
    a = jnp.exp(log_a)
    u = jnp.sqrt(-jnp.expm1(2.0 * log_a)) * (i * x.astype(jnp.float32))
    u = u.at[:, 0].add(a[:, 0] * h0.astype(jnp.float32))

    def combine(c1, c2):
        a1, b1 = c1
        a2, b2 = c2
        return a1 * a2, a2 * b1 + b2

    _, h = lax.associative_scan(combine, (a, u), axis=1)
    return h.astype(x.dtype), h[:, -1].astype(h0.dtype)


def diff_lambda(lq1, lk1, lq2, lk2, lambda_init):
    f32 = jnp.float32
    return (jnp.exp(jnp.sum(lq1.astype(f32) * lk1.astype(f32)))
            - jnp.exp(jnp.sum(lq2.astype(f32) * lk2.astype(f32))) + lambda_init)


def prompt_attend(q, k, v, lam):
    B, S = q.shape[:2]
    nb = S // Q_BLOCK
    qb = q.reshape(B, nb, Q_BLOCK, N_HEADS, 2, HEAD_DIM).swapaxes(0, 1)
    kpos = jnp.arange(S)
    scale = HEAD_DIM ** -0.5

    def one_block(args):
        qi, start = args
        s = jnp.einsum('bqhcd,bkhcd->bchqk', qi, k, preferred_element_type=jnp.float32) * scale
        qpos = start + jnp.arange(Q_BLOCK)
        s = jnp.where(kpos[None, :] <= qpos[:, None], s, -jnp.inf)
        p = jax.nn.softmax(s, axis=-1)
        att = (p[:, 0] - lam * p[:, 1]).astype(v.dtype)
        return jnp.einsum('bhqk,bkhe->bqhe', att, v)

    o = lax.map(one_block, (qb, jnp.arange(nb) * Q_BLOCK))
    return o.swapaxes(0, 1).reshape(B, S, N_HEADS, V_DIM)


def sample_attend(q, k, v, lam, k_past, v_past):
    T = q.shape[1]
    P = k_past.shape[1]
    scale = HEAD_DIM ** -0.5
    s_past = jnp.einsum('bqhcd,bkhcd->bchqk', q, k_past, preferred_element_type=jnp.float32) * scale
    s_new = jnp.einsum('bqhcd,bkhcd->bchqk', q, k, preferred_element_type=jnp.float32) * scale
    causal = jnp.arange(T)[None, :] <= jnp.arange(T)[:, None]
    s_new = jnp.where(causal, s_new, -jnp.inf)
    p = jax.nn.softmax(jnp.concatenate([s_past, s_new], axis=-1), axis=-1)
    att = (p[:, 0] - lam * p[:, 1]).astype(v.dtype)
    return (jnp.einsum('bhqk,bkhe->bqhe', att[..., :P], v_past)
            + jnp.einsum('bhqk,bkhe->bqhe', att[..., P:], v))


def trunk_layer(x, pos, conv_buf, h0, attend, l, ln1_g, w_in, conv_w, conv_b, rg_wa, rg_ba, rg_wx, rg_bx,
                rg_lambda, lam_q1, lam_k1, lam_q2, lam_k2, subln_g, w_branch, w_out, ln2_g, w_up, w_down):
    B, T, _ = x.shape
    lambda_init = 0.8 - 0.6 * math.exp(-0.3 * l)
    xn = rmsnorm(x, ln1_g[l])
    z = xn @ w_in[l]
    xr, gr, q, k, v, g_rnn, g_attn = jnp.split(z, SPLIT_POINTS, axis=-1)
    xc, new_buf = causal_conv(xr, conv_buf, conv_w[l], conv_b[l])
    h, h_last = rglru(xc, h0, rg_wa[l], rg_ba[l], rg_wx[l], rg_bx[l], rg_lambda[l])
    o_rnn = h * jax.nn.gelu(gr)
    q = rope(q.reshape(B, T, 2 * N_HEADS, HEAD_DIM), pos).reshape(B, T, N_HEADS, 2, HEAD_DIM)
    k = rope(k.reshape(B, T, 2 * N_HEADS, HEAD_DIM), pos).reshape(B, T, N_HEADS, 2, HEAD_DIM)
    v = v.reshape(B, T, N_HEADS, V_DIM)
    lam = diff_lambda(lam_q1[l], lam_k1[l], lam_q2[l], lam_k2[l], lambda_init)
    o = attend(q, k, v, lam)
    o_attn = (rmsnorm(o, subln_g[l], SUBLN_EPS) * (1.0 - lambda_init)).reshape(B, T, ATTN_WIDTH)
    y_rnn = o_rnn @ w_branch[l, :D_RNN]
    y_attn = o_attn @ w_branch[l, D_RNN:]
    merged = jax.nn.sigmoid(g_rnn) * y_rnn + jax.nn.sigmoid(g_attn) * y_attn
    x = x + merged @ w_out[l]
    hid = jax.nn.relu(rmsnorm(x, ln2_g[l]) @ w_up[l])
    x = x + (hid * hid) @ w_down[l]
    return x, k.reshape(B, T, N_HEADS, 2 * HEAD_DIM), v, h_last, new_buf


def setup_inputs(seed: int = 0) -> dict:
    key = jax.random.key(seed)
    ks = jax.random.split(key, 32)
    f32 = jnp.float32
    n_pages = PAST_LEN // PAGE_SIZE
    n_used = DEC_BATCH * n_pages
    n_pool = (5 * n_used + 3) // 4
    nrm = lambda k, shape, s: jax.random.normal(k, shape, f32) * s
    a0 = jax.random.uniform(ks[12], (DEPTH, D_RNN), f32, minval=0.9, maxval=0.999)
    a_base = a0 ** (1.0 / RG_C)
    rg_lambda = jnp.log(a_base) - jnp.log1p(-a_base)
    page_table = jax.random.permutation(ks[6], n_pool)[:n_used].reshape(DEC_BATCH, n_pages).astype(jnp.int32)
    return {
        'x_prompt': nrm(ks[0], (BATCH, SEQ, D_MODEL), 1.0),
        'x_sample': nrm(ks[1], (DEC_BATCH, DEC_SEQ, D_MODEL), 1.0),
        'cache_k': nrm(ks[2], (DEPTH, n_pool, PAGE_SIZE, N_HEADS, 2 * HEAD_DIM), 1.0),
        'cache_v': nrm(ks[3], (DEPTH, n_pool, PAGE_SIZE, N_HEADS, 2 * HEAD_DIM), 1.0),
        'state_h': nrm(ks[4], (DEPTH, DEC_BATCH, D_RNN), 1.0),
        'state_conv': nrm(ks[5], (DEPTH, DEC_BATCH, CONV_W - 1, D_RNN), 1.0),
        'page_table': page_table,
        'ln1_g': 1.0 + nrm(ks[7], (DEPTH, D_MODEL), 0.05),
        'w_in': nrm(ks[8], (DEPTH, D_MODEL, IN_WIDTH), D_MODEL ** -0.5),
        'conv_w': nrm(ks[9], (DEPTH, CONV_W, D_RNN), CONV_W ** -0.5),
        'conv_b': nrm(ks[10], (DEPTH, D_RNN), 0.05),
        'rg_wa': nrm(ks[11], (DEPTH, N_RG_BLOCKS, RG_BLOCK, RG_BLOCK), RG_BLOCK ** -0.5),
        'rg_ba': nrm(ks[13], (DEPTH, D_RNN), 0.1),
        'rg_wx': nrm(ks[14], (DEPTH, N_RG_BLOCKS, RG_BLOCK, RG_BLOCK), RG_BLOCK ** -0.5),
        'rg_bx': nrm(ks[15], (DEPTH, D_RNN), 0.1),
        'rg_lambda': rg_lambda,
        'lam_q1': nrm(ks[16], (DEPTH, HEAD_DIM), 0.1),
        'lam_k1': nrm(ks[17], (DEPTH, HEAD_DIM), 0.1),
        'lam_q2': nrm(ks[18], (DEPTH, HEAD_DIM), 0.1),
        'lam_k2': nrm(ks[19], (DEPTH, HEAD_DIM), 0.1),
        'subln_g': 1.0 + nrm(ks[20], (DEPTH, V_DIM), 0.05),
        'w_branch': nrm(ks[21], (DEPTH, D_RNN + ATTN_WIDTH, D_MODEL), D_RNN ** -0.5),
        'w_out': nrm(ks[22], (DEPTH, D_MODEL, D_MODEL), D_MODEL ** -0.5),
        'ln2_g': 1.0 + nrm(ks[23], (DEPTH, D_MODEL), 0.05),
        'w_up': nrm(ks[24], (DEPTH, D_MODEL, D_FF), D_MODEL ** -0.5),
        'w_down': nrm(ks[25], (DEPTH, D_FF, D_MODEL), D_FF ** -0.5),
        'final_g': 1.0 + nrm(ks[26], (D_MODEL,), 0.05),
    }


def reference(x_prompt, x_sample, cache_k, cache_v, state_h, state_conv, page_table, ln1_g, w_in, conv_w,
              conv_b, rg_wa, rg_ba, rg_wx, rg_bx, rg_lambda, lam_q1, lam_k1, lam_q2, lam_k2, subln_g,
              w_branch, w_out, ln2_g, w_up, w_down, final_g):
    B, S, _ = x_prompt.shape
    DB, T, _ = x_sample.shape
    past_len = page_table.shape[1] * PAGE_SIZE
    pos_p = jnp.arange(S)
    pos_s = past_len + jnp.arange(T)
    weights = (ln1_g, w_in, conv_w, conv_b, rg_wa, rg_ba, rg_wx, rg_bx, rg_lambda, lam_q1, lam_k1, lam_q2,
               lam_k2, subln_g, w_branch, w_out, ln2_g, w_up, w_down)
    xp, xs = x_prompt, x_sample
    kp_l, vp_l, hp_l, cp_l, ks_l, vs_l, hs_l, cs_l = [], [], [], [], [], [], [], []
    for l in range(DEPTH):
        buf0 = jnp.zeros((B, CONV_W - 1, D_RNN), xp.dtype)
        h00 = jnp.zeros((B, D_RNN), xp.dtype)
        xp, kp, vp, hp, cp = trunk_layer(xp, pos_p, buf0, h00, prompt_attend, l, *weights)
        k_past = cache_k[l, page_table].reshape(DB, past_len, N_HEADS, 2, HEAD_DIM)
        v_past = cache_v[l, page_table].reshape(DB, past_len, N_HEADS, V_DIM)
        attend = functools.partial(sample_attend, k_past=k_past, v_past=v_past)
        xs, ksm, vsm, hsm, csm = trunk_layer(xs, pos_s, state_conv[l], state_h[l], attend, l, *weights)
        kp_l.append(kp); vp_l.append(vp); hp_l.append(hp); cp_l.append(cp)
        ks_l.append(ksm); vs_l.append(vsm); hs_l.append(hsm); cs_l.append(csm)
    y_prompt = rmsnorm(xp, final_g)
    y_sample = rmsnorm(xs, final_g)
    return (y_prompt, y_sample, jnp.stack(kp_l), jnp.stack(vp_l), jnp.stack(hp_l), jnp.stack(cp_l),
            jnp.stack(ks_l), jnp.stack(vs_l), jnp.stack(hs_l), jnp.stack(cs_l))
```

```python
import functools
import math

import jax
import jax.numpy as jnp
from jax import lax
from jax.experimental import pallas as pl
from jax.experimental.pallas import tpu as pltpu

F32 = jnp.float32
BF16 = jnp.bfloat16

D_MODEL = 1024
N_HEADS = 8
HEAD_DIM = 64
V_DIM = 2 * HEAD_DIM
PAGE_SIZE = 128
ROPE_THETA = 10000.0
SUBLN_EPS = 1e-5
N_RG_BLOCKS = 16
RG_BLOCK = D_MODEL // N_RG_BLOCKS
CONV_W = 4
RG_C = 8.0
D_FF = 4 * D_MODEL
NORM_EPS = 1e-6
N_IN_SEGMENTS = 7
ATTN_SCALE = HEAD_DIM ** -0.5

LANES = 128
SUBLANES = 8
MXU_DIM = 256
RG_GROUPS = D_MODEL // MXU_DIM
VMEM_LIMIT = 56 * 1024 * 1024

NEG_INF = float("-inf")


def _params(semantics):
    return pltpu.CompilerParams(dimension_semantics=semantics, vmem_limit_bytes=VMEM_LIMIT)


def _diff_lambda(lq1_ref, lk1_ref, lq2_ref, lk2_ref, lambda_init):
    s1 = jnp.sum(lq1_ref[...] * lk1_ref[...], axis=-1, keepdims=True)
    s2 = jnp.sum(lq2_ref[...] * lk2_ref[...], axis=-1, keepdims=True)
    return jnp.exp(s1) - jnp.exp(s2) + lambda_init


def _subln(o, g, lambda_init):
    ms = jnp.mean(o * o, axis=-1, keepdims=True)
    return (o * lax.rsqrt(ms + SUBLN_EPS) * g) * (1.0 - lambda_init)


def _in_proj_kernel(x_ref, g_ref, w_ref, cos_ref, sin_ref,
                    xr_ref, gg_ref, q_ref, ka_ref, va_ref, ko_ref, vo_ref, sr_ref, sa_ref, xn_ref, *, tm):
    j = pl.program_id(1)

    @pl.when(j == 0)
    def _():
        x = x_ref[...]
        ms = jnp.mean(x * x, axis=-1, keepdims=True)
        xn_ref[...] = (x * lax.rsqrt(ms + NORM_EPS) * g_ref[...]).astype(BF16)

    z = jnp.dot(xn_ref[...], w_ref[...], preferred_element_type=F32)

    def head_cols(h):
        return slice(h * V_DIM, (h + 1) * V_DIM)

    def rope_heads():
        cos = cos_ref[...]
        sin = sin_ref[...]
        lane = lax.broadcasted_iota(jnp.int32, cos.shape, 1)
        lower = (lane & (HEAD_DIM // 2)) == 0
        for h in range(N_HEADS):
            zh = z[:, head_cols(h)]
            partner = jnp.where(lower,
                                pltpu.roll(zh, LANES - HEAD_DIM // 2, 1),
                                pltpu.roll(zh, HEAD_DIM // 2, 1))
            yield h, zh * cos + partner * sin

    @pl.when(j == 0)
    def _():
        xr_ref[...] = z

    @pl.when(j == 1)
    def _():
        gg_ref[...] = jax.nn.gelu(z)

    @pl.when(j == 2)
    def _():
        for h, qh in rope_heads():
            q_ref[:, head_cols(h)] = (qh * ATTN_SCALE).astype(BF16)

    @pl.when(j == 3)
    def _():
        for h, kh in rope_heads():
            ka_ref[:, head_cols(h)] = kh.astype(BF16)
            ko_ref[pl.ds(h, tm, stride=N_HEADS), :] = kh

    @pl.when(j == 4)
    def _():
        va_ref[...] = z.astype(BF16)
        for h in range(N_HEADS):
            vo_ref[pl.ds(h, tm, stride=N_HEADS), :] = z[:, head_cols(h)]

    @pl.when(j == 5)
    def _():
        sr_ref[...] = jax.nn.sigmoid(z)

    @pl.when(j == 6)
    def _():
        sa_ref[...] = jax.nn.sigmoid(z)


def _in_proj(x, g, w_bf, cos_t, sin_t, tm):
    n = x.shape[0]
    n_tab = cos_t.shape[0] // tm
    row = lambda i, j: (i, 0)
    flat = pl.BlockSpec((tm, D_MODEL), row)
    cache = pl.BlockSpec((tm * N_HEADS, V_DIM), row)
    flat_f32 = jax.ShapeDtypeStruct((n, D_MODEL), F32)
    flat_bf16 = jax.ShapeDtypeStruct((n, D_MODEL), BF16)
    cache_f32 = jax.ShapeDtypeStruct((n * N_HEADS, V_DIM), F32)
    return pl.pallas_call(
        functools.partial(_in_proj_kernel, tm=tm),
        grid=(n // tm, N_IN_SEGMENTS),
        in_specs=[
            flat,
            pl.BlockSpec((1, D_MODEL), lambda i, j: (0, 0)),
            pl.BlockSpec((D_MODEL, D_MODEL), lambda i, j: (0, j)),
            pl.BlockSpec((tm, LANES), lambda i, j: (i % n_tab, 0)),
            pl.BlockSpec((tm, LANES), lambda i, j: (i % n_tab, 0)),
        ],
        out_specs=[flat, flat, flat, flat, flat, cache, cache, flat, flat],
        out_shape=[flat_f32, flat_f32, flat_bf16, flat_bf16, flat_bf16, cache_f32, cache_f32,
                   flat_f32, flat_f32],
        scratch_shapes=[pltpu.VMEM((tm, D_MODEL), BF16)],
        compiler_params=_params(("parallel", "arbitrary")),
        name="in_proj",
    )(x, g, w_bf, cos_t, sin_t)


def _rnn_kernel(xr_ref, gg_ref, h0_ref, c0_ref, cw_ref, cb_ref, wa_ref, ba_ref, wx_ref, bx_ref, lam_ref,
                o_ref, hl_ref, cl_ref, xe_ref, a_ref, b_ref, h_ref, *, T):
    t = pl.program_id(1)
    halo = CONV_W - 1

    @pl.when(t == 0)
    def _():
        xe_ref[SUBLANES - halo:SUBLANES, :] = c0_ref[...]
        h_ref[...] = h0_ref[...]

    @pl.when(t > 0)
    def _():
        xe_ref[SUBLANES - halo:SUBLANES, :] = xe_ref[T + SUBLANES - halo:T + SUBLANES, :]

    xe_ref[SUBLANES:SUBLANES + T, :] = xr_ref[...]

    xc = cb_ref[...]
    for j in range(CONV_W):
        s = SUBLANES - halo + j
        xc = xc + xe_ref[s:s + T, :] * cw_ref[j:j + 1, :]

    xcb = xc.astype(BF16)

    def gate(w_ref, bias_ref):
        parts = [jnp.dot(xcb[:, g * MXU_DIM:(g + 1) * MXU_DIM], w_ref[g], preferred_element_type=F32)
                 for g in range(RG_GROUPS)]
        return jax.nn.sigmoid(jnp.concatenate(parts, axis=1) + bias_ref[...])

    r = gate(wa_ref, ba_ref)
    i = gate(wx_ref, bx_ref)
    log_a = (-RG_C) * r * jax.nn.softplus(-lam_ref[...])
    a = jnp.exp(log_a)
    u = jnp.sqrt(1.0 - a * a) * (i * xc)

    a3 = a.reshape(T // SUBLANES, SUBLANES, D_MODEL)
    u3 = u.reshape(T // SUBLANES, SUBLANES, D_MODEL)
    row = lax.broadcasted_iota(jnp.int32, a3.shape, 1)
    d = 1
    while d < SUBLANES:
        keep = row >= d
        a_prev = jnp.where(keep, pltpu.roll(a3, d, 1), 1.0)
        u_prev = jnp.where(keep, pltpu.roll(u3, d, 1), 0.0)
        u3 = a3 * u_prev + u3
        a3 = a3 * a_prev
        d *= 2
    a_ref[...] = a3.reshape(T, D_MODEL)
    b_ref[...] = u3.reshape(T, D_MODEL)

    def body(g, h):
        r0 = pl.multiple_of(g * SUBLANES, SUBLANES)
        hh = b_ref[pl.ds(r0, SUBLANES), :] + a_ref[pl.ds(r0, SUBLANES), :] * h
        b_ref[pl.ds(r0, SUBLANES), :] = hh
        return hh[SUBLANES - 1:SUBLANES, :]

    h = lax.fori_loop(0, T // SUBLANES, body, h_ref[...])
    h_ref[...] = h
    hl_ref[...] = h
    cl_ref[...] = xe_ref[T + SUBLANES - halo:T + SUBLANES, :]
    o_ref[...] = (b_ref[...] * gg_ref[...]).astype(o_ref.dtype)


def _rnn(xr, gg, h0, c0, cw, cb, wa4, ba, wx4, bx, lam, T):
    B, S, _ = xr.shape
    assert S % T == 0 and T % SUBLANES == 0 and T >= CONV_W - 1
    seq = lambda b, t: (b, t, 0)
    per_b = lambda b, t: (b, 0, 0)
    vec = pl.BlockSpec((1, D_MODEL), lambda b, t: (0, 0))
    wspec = pl.BlockSpec((RG_GROUPS, MXU_DIM, MXU_DIM), lambda b, t: (0, 0, 0))
    return pl.pallas_call(
        functools.partial(_rnn_kernel, T=T),
        grid=(B, S // T),
        in_specs=[
            pl.BlockSpec((None, T, D_MODEL), seq),
            pl.BlockSpec((None, T, D_MODEL), seq),
            pl.BlockSpec((None, 1, D_MODEL), per_b),
            pl.BlockSpec((None, CONV_W - 1, D_MODEL), per_b),
            pl.BlockSpec((CONV_W, D_MODEL), lambda b, t: (0, 0)),
            vec, wspec, vec, wspec, vec, vec,
        ],
        out_specs=[
            pl.BlockSpec((None, T, D_MODEL), seq),
            pl.BlockSpec((None, 1, D_MODEL), per_b),
            pl.BlockSpec((None, CONV_W - 1, D_MODEL), per_b),
        ],
        out_shape=[
            jax.ShapeDtypeStruct((B, S, D_MODEL), BF16),
            jax.ShapeDtypeStruct((B, 1, D_MODEL), F32),
            jax.ShapeDtypeStruct((B, CONV_W - 1, D_MODEL), F32),
        ],
        scratch_shapes=[
            pltpu.VMEM((T + SUBLANES, D_MODEL), F32),
            pltpu.VMEM((T, D_MODEL), F32),
            pltpu.VMEM((T, D_MODEL), F32),
            pltpu.VMEM((1, D_MODEL), F32),
        ],
        compiler_params=_params(("parallel", "arbitrary")),
        name="rnn",
    )(xr, gg, h0, c0, cw, cb, wa4, ba, wx4, bx, lam)


def _online_update(c, s, pv_fn, m_ref, l_ref, acc_ref):
    m_old = m_ref[c]
    m_new = jnp.maximum(m_old, jnp.max(s, axis=-1, keepdims=True))
    alpha = jnp.exp(m_old - m_new)
    p = jnp.exp(s - m_new)
    l_ref[c] = alpha * l_ref[c] + jnp.sum(p, axis=-1, keepdims=True)
    acc_ref[c] = alpha * acc_ref[c] + pv_fn(p.astype(BF16))
    m_ref[c] = m_new


_NT = (((1,), (1,)), ((), ()))


def _attn_p_kernel(q_ref, k_ref, v_ref, lq1_ref, lk1_ref, lq2_ref, lk2_ref, g_ref,
                   o_ref, m_ref, l_ref, acc_ref, *, tq, lambda_init):
    qi = pl.program_id(2)
    q = q_ref[...]
    lane = lax.broadcasted_iota(jnp.int32, q.shape, 1)
    zero = jnp.zeros_like(q)
    qs = (jnp.where(lane < HEAD_DIM, q, zero), jnp.where(lane >= HEAD_DIM, q, zero))
    m_ref[...] = jnp.full(m_ref.shape, NEG_INF, F32)
    l_ref[...] = jnp.zeros(l_ref.shape, F32)
    acc_ref[...] = jnp.zeros(acc_ref.shape, F32)

    def chunk(kc, masked):
        r0 = pl.multiple_of(kc * tq, tq)
        kb = k_ref[pl.ds(r0, tq), :]
        vb = v_ref[pl.ds(r0, tq), :]
        pv = lambda p: jnp.dot(p, vb, preferred_element_type=F32)
        for c in range(2):
            s = lax.dot_general(qs[c], kb, _NT, preferred_element_type=F32)
            if masked:
                rr = lax.broadcasted_iota(jnp.int32, s.shape, 0)
                cc = lax.broadcasted_iota(jnp.int32, s.shape, 1)
                s = jnp.where(cc <= rr, s, NEG_INF)
            _online_update(c, s, pv, m_ref, l_ref, acc_ref)

    def body(kc, carry):
        chunk(kc, False)
        return carry

    lax.fori_loop(0, qi, body, 0)
    chunk(qi, True)

    lam = _diff_lambda(lq1_ref, lk1_ref, lq2_ref, lk2_ref, lambda_init)
    o = acc_ref[0] / l_ref[0] - lam * (acc_ref[1] / l_ref[1])
    o_ref[...] = _subln(o, g_ref[...], lambda_init).astype(o_ref.dtype)


def _attn_prompt(q, k, v, lq1, lk1, lq2, lk2, g, lambda_init, tq):
    B, S, _ = q.shape
    lvec = pl.BlockSpec((1, HEAD_DIM), lambda b, h, i: (0, 0))
    kv = pl.BlockSpec((None, S, V_DIM), lambda b, h, i: (b, 0, h))
    return pl.pallas_call(
        functools.partial(_attn_p_kernel, tq=tq, lambda_init=lambda_init),
        grid=(B, N_HEADS, S // tq),
        in_specs=[
            pl.BlockSpec((None, tq, V_DIM), lambda b, h, i: (b, i, h)),
            kv, kv, lvec, lvec, lvec, lvec,
            pl.BlockSpec((1, V_DIM), lambda b, h, i: (0, 0)),
        ],
        out_specs=pl.BlockSpec((None, tq, V_DIM), lambda b, h, i: (b, i, h)),
        out_shape=jax.ShapeDtypeStruct((B, S, D_MODEL), BF16),
        scratch_shapes=[
            pltpu.VMEM((2, tq, 1), F32),
            pltpu.VMEM((2, tq, 1), F32),
            pltpu.VMEM((2, tq, V_DIM), F32),
        ],
        compiler_params=_params(("parallel", "parallel", "arbitrary")),
        name="attn_prompt",
    )(q, k, v, lq1, lk1, lq2, lk2, g)


def _attn_s_kernel(pt_ref, q_ref, kn_ref, vn_ref, lq1_ref, lk1_ref, lq2_ref, lk2_ref, g_ref, *rest,
                   pp, T, lambda_init):
    k_refs = rest[:pp]
    v_refs = rest[pp:2 * pp]
    o_ref, qm_ref, m_ref, l_ref, acc_ref = rest[2 * pp:]
    del pt_ref
    p = pl.program_id(1)
    n_rows = N_HEADS * 2 * T
    page_rows = PAGE_SIZE * N_HEADS

    @pl.when(p == 0)
    def _():
        q = q_ref[...].astype(F32)
        lane = lax.broadcasted_iota(jnp.int32, (T, V_DIM), 1)
        blocks = []
        for h in range(N_HEADS):
            qh = q[:, h * V_DIM:(h + 1) * V_DIM]
            blocks.append(jnp.where(lane < HEAD_DIM, qh, 0.0))
            blocks.append(jnp.where(lane >= HEAD_DIM, qh, 0.0))
        qm_ref[...] = jnp.concatenate(blocks, axis=0).astype(BF16)
        m_ref[...] = jnp.full(m_ref.shape, NEG_INF, F32)
        l_ref[...] = jnp.zeros(l_ref.shape, F32)
        acc_ref[...] = jnp.zeros(acc_ref.shape, F32)

    qm = qm_ref[...]

    def scores(kb):
        return lax.dot_general(qm, kb, _NT, preferred_element_type=F32)

    def fold(s, v_list):
        width = s.shape[1] // len(v_list)

        def pv(pb):
            out = None
            for i, vb in enumerate(v_list):
                part = jnp.dot(pb[:, i * width:(i + 1) * width], vb, preferred_element_type=F32)
                out = part if out is None else out + part
            return out
        _online_update(0, s, pv, m_ref, l_ref, acc_ref)

    rr = lax.broadcasted_iota(jnp.int32, (n_rows, page_rows), 0)
    cc = lax.broadcasted_iota(jnp.int32, (n_rows, page_rows), 1)
    same_head = (cc % N_HEADS) == (rr // (2 * T))
    s_past = jnp.concatenate(
        [jnp.where(same_head, scores(k_refs[i][...].astype(BF16)), NEG_INF) for i in range(pp)], axis=1)
    fold(s_past, [v_refs[i][...].astype(BF16) for i in range(pp)])

    @pl.when(p == pl.num_programs(1) - 1)
    def _():
        new_rows = T * N_HEADS
        pad = jnp.zeros((LANES - new_rows, V_DIM), F32)
        kn = jnp.concatenate([kn_ref[...], pad], axis=0).astype(BF16)
        vn = jnp.concatenate([vn_ref[...], pad], axis=0).astype(BF16)
        s = scores(kn)
        r2 = lax.broadcasted_iota(jnp.int32, s.shape, 0)
        c2 = lax.broadcasted_iota(jnp.int32, s.shape, 1)
        s = jnp.where((c2 % N_HEADS) == (r2 // (2 * T)), s, NEG_INF)
        s = jnp.where(c2 // N_HEADS <= r2 % T, s, NEG_INF)
        fold(s, [vn])

        lam = _diff_lambda(lq1_ref, lk1_ref, lq2_ref, lk2_ref, lambda_init)
        accn = acc_ref[0] / l_ref[0]
        for h in range(N_HEADS):
            r0 = h * 2 * T
            dh = accn[r0:r0 + T] - lam * accn[r0 + T:r0 + 2 * T]
            o_ref[:, h * V_DIM:(h + 1) * V_DIM] = _subln(dh, g_ref[...], lambda_init).astype(o_ref.dtype)


def _attn_sample(q, k_new, v_new, cache_k, cache_v, layer, page_table, lq1, lk1, lq2, lk2, g,
                 lambda_init, pp):
    DB, T, _ = q.shape
    n_pages = page_table.shape[1]
    assert n_pages % pp == 0 and T == SUBLANES and T * N_HEADS <= LANES
    n_rows = N_HEADS * 2 * T
    pt_flat = page_table.reshape(-1)
    seq = pl.BlockSpec((None, T, D_MODEL), lambda b, p, pt: (b, 0, 0))
    new = pl.BlockSpec((None, T * N_HEADS, V_DIM), lambda b, p, pt: (b, 0, 0))
    lvec = pl.BlockSpec((1, HEAD_DIM), lambda b, p, pt: (0, 0))

    def page_spec(i):
        return pl.BlockSpec((None, None, PAGE_SIZE * N_HEADS, V_DIM),
                            lambda b, p, pt: (layer, pt[b * n_pages + p * pp + i], 0, 0))

    grid_spec = pltpu.PrefetchScalarGridSpec(
        num_scalar_prefetch=1,
        grid=(DB, n_pages // pp),
        in_specs=[seq, new, new, lvec, lvec, lvec, lvec,
                  pl.BlockSpec((1, V_DIM), lambda b, p, pt: (0, 0))]
                 + [page_spec(i) for i in range(pp)] * 2,
        out_specs=seq,
        scratch_shapes=[
            pltpu.VMEM((n_rows, V_DIM), BF16),
            pltpu.VMEM((1, n_rows, 1), F32),
            pltpu.VMEM((1, n_rows, 1), F32),
            pltpu.VMEM((1, n_rows, V_DIM), F32),
        ],
    )
    return pl.pallas_call(
        functools.partial(_attn_s_kernel, pp=pp, T=T, lambda_init=lambda_init),
        grid_spec=grid_spec,
        out_shape=jax.ShapeDtypeStruct((DB, T, D_MODEL), BF16),
        compiler_params=_params(("parallel", "arbitrary")),
        name="attn_sample",
    )(pt_flat, q, k_new, v_new, lq1, lk1, lq2, lk2, g, *([cache_k] * pp), *([cache_v] * pp))


def _merge_kernel(orn_ref, oat_ref, sr_ref, sa_ref, x_ref, wbr_ref, wba_ref, wo_ref, o_ref):
    y_rnn = jnp.dot(orn_ref[...], wbr_ref[...], preferred_element_type=F32)
    y_attn = jnp.dot(oat_ref[...], wba_ref[...], preferred_element_type=F32)
    merged = sr_ref[...] * y_rnn + sa_ref[...] * y_attn
    o_ref[...] = x_ref[...] + jnp.dot(merged.astype(BF16), wo_ref[...], preferred_element_type=F32)


def _merge(o_rnn, o_attn, sr, sa, x, wbr, wba, wo, tm):
    n = x.shape[0]
    row = pl.BlockSpec((tm, D_MODEL), lambda i: (i, 0))
    wspec = pl.BlockSpec((D_MODEL, D_MODEL), lambda i: (0, 0))
    return pl.pallas_call(
        _merge_kernel,
        grid=(n // tm,),
        in_specs=[row, row, row, row, row, wspec, wspec, wspec],
        out_specs=row,
        out_shape=jax.ShapeDtypeStruct((n, D_MODEL), F32),
        compiler_params=_params(("parallel",)),
        name="merge",
    )(o_rnn, o_attn, sr, sa, x, wbr, wba, wo)


def _mlp_kernel(x_ref, g_ref, wu_ref, wd_ref, gf_ref, o_ref, *, final):
    x = x_ref[...]
    ms = jnp.mean(x * x, axis=-1, keepdims=True)
    xn = (x * lax.rsqrt(ms + NORM_EPS) * g_ref[...]).astype(BF16)
    acc = x
    for c in range(D_FF // D_MODEL):
        cols = slice(c * D_MODEL, (c + 1) * D_MODEL)
        hid = jnp.maximum(jnp.dot(xn, wu_ref[:, cols], preferred_element_type=F32), 0.0)
        acc = acc + jnp.dot((hid * hid).astype(BF16), wd_ref[cols, :], preferred_element_type=F32)
    if final:
        ms = jnp.mean(acc * acc, axis=-1, keepdims=True)
        acc = acc * lax.rsqrt(ms + NORM_EPS) * gf_ref[...]
    o_ref[...] = acc


def _mlp(x, g, wu, wd, gf, final, tm):
    n = x.shape[0]
    row = pl.BlockSpec((tm, D_MODEL), lambda i: (i, 0))
    vec = pl.BlockSpec((1, D_MODEL), lambda i: (0, 0))
    return pl.pallas_call(
        functools.partial(_mlp_kernel, final=final),
        grid=(n // tm,),
        in_specs=[row, vec,
                  pl.BlockSpec((D_MODEL, D_FF), lambda i: (0, 0), pipeline_mode=pl.Buffered(1)),
                  pl.BlockSpec((D_FF, D_MODEL), lambda i: (0, 0), pipeline_mode=pl.Buffered(1)),
                  vec],
        out_specs=row,
        out_shape=jax.ShapeDtypeStruct((n, D_MODEL), F32),
        compiler_params=_params(("parallel",)),
        name="mlp",
    )(x, g, wu, wd, gf)


def _rope_tables(pos):
    half = HEAD_DIM // 2
    inv = ROPE_THETA ** (-2.0 * jnp.arange(half, dtype=F32) / HEAD_DIM)
    ang = pos.astype(F32)[:, None] * inv[None, :]
    cos = jnp.cos(ang)
    sin = jnp.sin(ang)
    reps = LANES // HEAD_DIM
    return (jnp.concatenate([cos, cos] * reps, axis=1),
            jnp.concatenate([-sin, sin] * reps, axis=1))


def _gate_tiles(w):
    per = MXU_DIM // RG_BLOCK
    w5 = w.reshape(RG_GROUPS, per, RG_BLOCK, RG_BLOCK)
    eye = jnp.eye(per, dtype=w.dtype)
    t = w5[:, :, :, None, :] * eye[None, :, None, :, None]
    return t.reshape(RG_GROUPS, MXU_DIM, MXU_DIM).astype(BF16)


def kernel(x_prompt, x_sample, cache_k, cache_v, state_h, state_conv, page_table, ln1_g, w_in, conv_w, conv_b, rg_wa, rg_ba, rg_wx, rg_bx, rg_lambda, lam_q1, lam_k1, lam_q2, lam_k2, subln_g, w_branch, w_out, ln2_g, w_up, w_down, final_g):
    B, S, _ = x_prompt.shape
    DB, T, _ = x_sample.shape
    depth = w_in.shape[0]
    n_pool = cache_k.shape[1]
    past_len = page_table.shape[1] * PAGE_SIZE

    cos_p, sin_p = _rope_tables(jnp.arange(S))
    cos_s, sin_s = _rope_tables(past_len + jnp.arange(DB * T) % T)
    ck = cache_k.reshape(depth, n_pool, PAGE_SIZE * N_HEADS, V_DIM)
    cv = cache_v.reshape(depth, n_pool, PAGE_SIZE * N_HEADS, V_DIM)

    xp = x_prompt.reshape(B * S, D_MODEL)
    xs = x_sample.reshape(DB * T, D_MODEL)
    h0_p = jnp.zeros((B, 1, D_MODEL), F32)
    c0_p = jnp.zeros((B, CONV_W - 1, D_MODEL), F32)
    gf = final_g.reshape(1, D_MODEL)

    outs = [[] for _ in range(8)]
    for l in range(depth):
        lambda_init = 0.8 - 0.6 * math.exp(-0.3 * l)
        w_in_bf = w_in[l].astype(BF16)
        wbr = w_branch[l, :D_MODEL].astype(BF16)
        wba = w_branch[l, D_MODEL:].astype(BF16)
        wo = w_out[l].astype(BF16)
        wu = w_up[l].astype(BF16)
        wd = w_down[l].astype(BF16)
        wa4 = _gate_tiles(rg_wa[l])
        wx4 = _gate_tiles(rg_wx[l])
        vec = lambda a: a[l].reshape(1, -1)
        lam_vecs = (vec(lam_q1), vec(lam_k1), vec(lam_q2), vec(lam_k2))
        final = l == depth - 1

        def layer(x, n_seq, seq_len, cos_t, sin_t, h0, c0, attend, tm, t_rnn):
            xr, gg, q, ka, va, ko, vo, sr, sa = _in_proj(x, vec(ln1_g), w_in_bf, cos_t, sin_t, tm)
            as_seq = lambda a: a.reshape(n_seq, seq_len, D_MODEL)
            o_rnn, h_last, c_last = _rnn(as_seq(xr), as_seq(gg), h0, c0, conv_w[l], vec(conv_b), wa4,
                                         vec(rg_ba), wx4, vec(rg_bx), vec(rg_lambda), t_rnn)
            o_attn = attend(as_seq(q), as_seq(ka), as_seq(va),
                            ko.reshape(n_seq, seq_len * N_HEADS, V_DIM), vo.reshape(n_seq, seq_len * N_HEADS, V_DIM))
            x1 = _merge(o_rnn.reshape(-1, D_MODEL), o_attn.reshape(-1, D_MODEL), sr, sa, x, wbr, wba, wo, tm)
            x2 = _mlp(x1, vec(ln2_g), wu, wd, gf, final, tm)
            return (x2, ko.reshape(n_seq, seq_len, N_HEADS, V_DIM), vo.reshape(n_seq, seq_len, N_HEADS, V_DIM),
                    h_last.reshape(n_seq, D_MODEL), c_last)

        attend_p = lambda q, ka, va, ko, vo: _attn_prompt(q, ka, va, *lam_vecs, vec(subln_g), lambda_init, 256)
        attend_s = lambda q, ka, va, ko, vo: _attn_sample(q, ko, vo, ck, cv, l, page_table, *lam_vecs,
                                                          vec(subln_g), lambda_init, 8)
        xp, kp, vp, hp, cp = layer(xp, B, S, cos_p, sin_p, h0_p, c0_p, attend_p, 512, 256)
        xs, ks, vs, hs, cs = layer(xs, DB, T, cos_s, sin_s, state_h[l].reshape(DB, 1, D_MODEL),
                                   state_conv[l], attend_s, DB * T, T)
        for lst, val in zip(outs, (kp, vp, hp, cp, ks, vs, hs, cs)):
            lst.append(val)

    return (xp.reshape(B, S, D_MODEL), xs.reshape(DB, T, D_MODEL)) + tuple(jnp.stack(o) for o in outs)
```

```python
import functools
import math

import jax
import jax.numpy as jnp
from jax import lax
from jax.experimental import pallas as pl
from jax.experimental.pallas import tpu as pltpu

F32 = jnp.float32
BF16 = jnp.bfloat16

D_MODEL = 1024
N_HEADS = 8
HEAD_DIM = 64
V_DIM = 2 * HEAD_DIM
PAGE_SIZE = 128
ROPE_THETA = 10000.0
SUBLN_EPS = 1e-5
N_RG_BLOCKS = 16
RG_BLOCK = D_MODEL // N_RG_BLOCKS
CONV_W = 4
RG_C = 8.0
D_FF = 4 * D_MODEL
NORM_EPS = 1e-6
ATTN_SCALE = HEAD_DIM ** -0.5
SEG_XR, SEG_GR, SEG_Q, SEG_K, SEG_V, SEG_GRNN, SEG_GATTN = range(7)

LANES = 128
SUBLANES = 8
MXU_DIM = 256
RG_GROUPS = D_MODEL // MXU_DIM
VMEM_LIMIT = 58 * 1024 * 1024

NEG_INF = float("-inf")
_NT = (((1,), (1,)), ((), ()))


def _params(semantics):
    return pltpu.CompilerParams(dimension_semantics=semantics, vmem_limit_bytes=VMEM_LIMIT)


def _const_spec(shape, n_grid):
    zeros = (0,) * len(shape)
    return pl.BlockSpec(shape, lambda *_: zeros, pipeline_mode=pl.Buffered(1))


def _rmsnorm_bf16(x, g):
    ms = jnp.mean(x * x, axis=-1, keepdims=True)
    return (x * lax.rsqrt(ms + NORM_EPS) * g).astype(BF16)


def _diff_lambda(lq1_ref, lk1_ref, lq2_ref, lk2_ref, lambda_init):
    s1 = jnp.sum(lq1_ref[...] * lk1_ref[...], axis=-1, keepdims=True)
    s2 = jnp.sum(lq2_ref[...] * lk2_ref[...], axis=-1, keepdims=True)
    return jnp.exp(s1) - jnp.exp(s2) + lambda_init


def _subln(o, g, lambda_init):
    ms = jnp.mean(o * o, axis=-1, keepdims=True)
    return (o * lax.rsqrt(ms + SUBLN_EPS) * g) * (1.0 - lambda_init)


def _online_update(c, s, pv_fn, m_ref, l_ref, acc_ref, axis):
    m_old = m_ref[c]
    m_new = jnp.maximum(m_old, jnp.max(s, axis=axis, keepdims=True))
    alpha = jnp.exp(m_old - m_new)
    p = jnp.exp(s - m_new)
    l_ref[c] = alpha * l_ref[c] + jnp.sum(p, axis=axis, keepdims=True)
    acc_ref[c] = alpha * acc_ref[c] + pv_fn(p.astype(BF16))
    m_ref[c] = m_new


def _qkv_kernel(x_ref, g_ref, wq_ref, wk_ref, wv_ref, cos_ref, sin_ref, *out_refs, tm, transposed):
    if transposed:
        q_ref, ka_ref, va_ref, ko_ref, vo_ref = out_refs
    else:
        q_ref, ko_ref, vo_ref = out_refs
    xn = _rmsnorm_bf16(x_ref[...], g_ref[...])
    cos = cos_ref[...]
    sin = sin_ref[...]
    lane = lax.broadcasted_iota(jnp.int32, cos.shape, 1)
    lower = (lane & (HEAD_DIM // 2)) == 0

    def head_cols(h):
        return slice(h * V_DIM, (h + 1) * V_DIM)

    def rope_heads(z):
        for h in range(N_HEADS):
            zh = z[:, head_cols(h)]
            partner = jnp.where(lower,
                                pltpu.roll(zh, LANES - HEAD_DIM // 2, 1),
                                pltpu.roll(zh, HEAD_DIM // 2, 1))
            yield h, zh * cos + partner * sin

    zq = jnp.dot(xn, wq_ref[...], preferred_element_type=F32)
    for h, qh in rope_heads(zq):
        qh = qh * ATTN_SCALE
        if transposed:
            q_ref[head_cols(h), :] = qh.T.astype(BF16)
        else:
            q_ref[:, head_cols(h)] = qh.astype(BF16)

    zk = jnp.dot(xn, wk_ref[...], preferred_element_type=F32)
    for h, kh in rope_heads(zk):
        ko_ref[pl.ds(h, tm, stride=N_HEADS), :] = kh
        if transposed:
            ka_ref[:, head_cols(h)] = kh.astype(BF16)

    zv = jnp.dot(xn, wv_ref[...], preferred_element_type=F32)
    for h in range(N_HEADS):
        vh = zv[:, head_cols(h)]
        vo_ref[pl.ds(h, tm, stride=N_HEADS), :] = vh
        if transposed:
            va_ref[head_cols(h), :] = vh.T.astype(BF16)


def _qkv(x, g, wq, wk, wv, cos_t, sin_t, tm, transposed):
    n = x.shape[0]
    n_tab = cos_t.shape[0] // tm
    row = lambda i: (i, 0)
    flat = pl.BlockSpec((tm, D_MODEL), row)
    chan = pl.BlockSpec((D_MODEL, tm), lambda i: (0, i))
    cache = pl.BlockSpec((tm * N_HEADS, V_DIM), row)
    tab = pl.BlockSpec((tm, LANES), lambda i: (i % n_tab, 0))
    w = _const_spec((D_MODEL, D_MODEL), 1)
    flat_bf16 = jax.ShapeDtypeStruct((n, D_MODEL), BF16)
    chan_bf16 = jax.ShapeDtypeStruct((D_MODEL, n), BF16)
    cache_f32 = jax.ShapeDtypeStruct((n * N_HEADS, V_DIM), F32)
    if transposed:
        out_specs = [chan, flat, chan, cache, cache]
        out_shape = [chan_bf16, flat_bf16, chan_bf16, cache_f32, cache_f32]
    else:
        out_specs = [flat, cache, cache]
        out_shape = [flat_bf16, cache_f32, cache_f32]
    return pl.pallas_call(
        functools.partial(_qkv_kernel, tm=tm, transposed=transposed),
        grid=(n // tm,),
        in_specs=[flat, _const_spec((1, D_MODEL), 1), w, w, w, tab, tab],
        out_specs=out_specs,
        out_shape=out_shape,
        compiler_params=_params(("parallel",)),
        name="qkv",
    )(x, g, wq, wk, wv, cos_t, sin_t)


def _rnn_kernel(x_ref, g_ref, wxr_ref, wgr_ref, h0_ref, c0_ref, cw_ref, cb_ref,
                wa_ref, ba_ref, wx_ref, bx_ref, lam_ref,
                o_ref, hl_ref, cl_ref,
                xe_ref, a_ref, u_ref, gg_ref, al_ref, bl_ref, o32_ref, h_ref, *, T):
    t = pl.program_id(1)
    halo = CONV_W - 1
    G = T // SUBLANES
    NB = D_MODEL // LANES

    @pl.when(t == 0)
    def _():
        xe_ref[SUBLANES - halo:SUBLANES, :] = c0_ref[...]
        h_ref[...] = h0_ref[...]

    @pl.when(t > 0)
    def _():
        xe_ref[SUBLANES - halo:SUBLANES, :] = xe_ref[T + SUBLANES - halo:T + SUBLANES, :]

    def to_blocks(ref, val):
        for c in range(NB):
            ref[c] = val[:, c * LANES:(c + 1) * LANES]

    xn = _rmsnorm_bf16(x_ref[...], g_ref[...])
    xe_ref[SUBLANES:SUBLANES + T, :] = jnp.dot(xn, wxr_ref[...], preferred_element_type=F32)
    to_blocks(gg_ref, jax.nn.gelu(jnp.dot(xn, wgr_ref[...], preferred_element_type=F32)))

    xc = cb_ref[...]
    for j in range(CONV_W):
        s = SUBLANES - halo + j
        xc = xc + xe_ref[s:s + T, :] * cw_ref[j:j + 1, :]

    xcb = xc.astype(BF16)

    def gate(w_ref, bias_ref):
        parts = [jnp.dot(xcb[:, g * MXU_DIM:(g + 1) * MXU_DIM], w_ref[g], preferred_element_type=F32)
                 for g in range(RG_GROUPS)]
        return jax.nn.sigmoid(jnp.concatenate(parts, axis=1) + bias_ref[...])

    r = gate(wa_ref, ba_ref)
    i = gate(wx_ref, bx_ref)
    log_a = r * ((-RG_C) * jax.nn.softplus(-lam_ref[...]))
    a = jnp.exp(log_a)
    to_blocks(a_ref, a)
    to_blocks(u_ref, jnp.sqrt(1.0 - a * a) * (i * xc))

    def rows(ref, c, k):
        return ref[c, pl.ds(k, G, stride=SUBLANES), :]

    for c in range(NB):
        cols = slice(c * LANES, (c + 1) * LANES)
        a_cum = rows(a_ref, c, 0)
        b_cum = rows(u_ref, c, 0)
        al_ref[0, c] = a_cum
        bl_ref[0, c] = b_cum
        for k in range(1, SUBLANES):
            ak = rows(a_ref, c, k)
            b_cum = ak * b_cum + rows(u_ref, c, k)
            a_cum = ak * a_cum
            al_ref[k, c] = a_cum
            bl_ref[k, c] = b_cum
        h = h_ref[:, cols]
        for gi in range(G):
            a_ref[c, gi:gi + 1, :] = h
            h = bl_ref[SUBLANES - 1, c, gi:gi + 1, :] + al_ref[SUBLANES - 1, c, gi:gi + 1, :] * h
        h_ref[:, cols] = h
        h_in = a_ref[c, 0:G, :]
        for k in range(SUBLANES):
            hk = bl_ref[k, c] + al_ref[k, c] * h_in
            o32_ref[c, pl.ds(k, G, stride=SUBLANES), :] = hk * rows(gg_ref, c, k)
        o_ref[:, cols] = o32_ref[c].astype(o_ref.dtype)
    hl_ref[...] = h_ref[...]
    cl_ref[...] = xe_ref[T + SUBLANES - halo:T + SUBLANES, :]


def _rnn(x, g, wxr, wgr, h0, c0, cw, cb, wa4, ba, wx4, bx, lam, T):
    B, S, _ = x.shape
    assert S % T == 0 and T % SUBLANES == 0 and T >= CONV_W - 1
    G = T // SUBLANES
    NB = D_MODEL // LANES
    seq = lambda b, t: (b, t, 0)
    per_b = lambda b, t: (b, 0, 0)
    vec = _const_spec((1, D_MODEL), 2)
    w = _const_spec((D_MODEL, D_MODEL), 2)
    wgate = _const_spec((RG_GROUPS, MXU_DIM, MXU_DIM), 2)
    tile = pltpu.VMEM((NB, T, LANES), F32)
    return pl.pallas_call(
        functools.partial(_rnn_kernel, T=T),
        grid=(B, S // T),
        in_specs=[
            pl.BlockSpec((None, T, D_MODEL), seq),
            vec, w, w,
            pl.BlockSpec((None, 1, D_MODEL), per_b),
            pl.BlockSpec((None, CONV_W - 1, D_MODEL), per_b),
            _const_spec((CONV_W, D_MODEL), 2),
            vec, wgate, vec, wgate, vec, vec,
        ],
        out_specs=[
            pl.BlockSpec((None, T, D_MODEL), seq),
            pl.BlockSpec((None, 1, D_MODEL), per_b),
            pl.BlockSpec((None, CONV_W - 1, D_MODEL), per_b),
        ],
        out_shape=[
            jax.ShapeDtypeStruct((B, S, D_MODEL), BF16),
            jax.ShapeDtypeStruct((B, 1, D_MODEL), F32),
            jax.ShapeDtypeStruct((B, CONV_W - 1, D_MODEL), F32),
        ],
        scratch_shapes=[
            pltpu.VMEM((T + SUBLANES, D_MODEL), F32),
            tile, tile, tile,
            pltpu.VMEM((SUBLANES, NB, G, LANES), F32),
            pltpu.VMEM((SUBLANES, NB, G, LANES), F32),
            tile,
            pltpu.VMEM((1, D_MODEL), F32),
        ],
        compiler_params=_params(("parallel", "arbitrary")),
        name="rnn",
    )(x, g, wxr, wgr, h0, c0, cw, cb, wa4, ba, wx4, bx, lam)


def _attn_p_kernel(qt_ref, k_ref, vt_ref, lq1_ref, lk1_ref, lq2_ref, lk2_ref, g_ref,
                   o_ref, q2_ref, s_ref, p_ref, alpha_ref, m_ref, l_ref, acc_ref, *, tq, tk, lambda_init):
    qi = pl.program_id(2)
    n_pairs = qi

    qt = qt_ref[...]
    row = lax.broadcasted_iota(jnp.int32, qt.shape, 0)
    zero = jnp.zeros_like(qt)
    q2_ref[:, :tq] = jnp.where(row < HEAD_DIM, qt, zero)
    q2_ref[:, tq:] = jnp.where(row >= HEAD_DIM, qt, zero)
    m_ref[...] = jnp.full(m_ref.shape, NEG_INF, F32)
    l_ref[...] = jnp.zeros(l_ref.shape, F32)
    acc_ref[...] = jnp.zeros(acc_ref.shape, F32)
    p_ref[1] = jnp.zeros(p_ref.shape[1:], BF16)
    alpha_ref[1] = jnp.ones(alpha_ref.shape[1:], F32)

    def scores(kc):
        r0 = pl.multiple_of(kc * tk, tk)
        return jnp.dot(k_ref[pl.ds(r0, tk), :], q2_ref[...], preferred_element_type=F32)

    def apply_values(kc, slot):
        r0 = pl.multiple_of(kc * tk, tk)
        pv = jnp.dot(vt_ref[:, pl.ds(r0, tk)], p_ref[slot], preferred_element_type=F32)
        acc_ref[...] = alpha_ref[slot] * acc_ref[...] + pv

    def softmax(kc, slot, masked):
        s = s_ref[slot]
        if masked:
            key = kc * tk + lax.broadcasted_iota(jnp.int32, s.shape, 0)
            qry = qi * tq + (lax.broadcasted_iota(jnp.int32, s.shape, 1) & (tq - 1))
            s = jnp.where(key <= qry, s, NEG_INF)
        m_old = m_ref[...]
        m_new = jnp.maximum(m_old, jnp.max(s, axis=0, keepdims=True))
        alpha = jnp.exp(m_old - m_new)
        p = jnp.exp(s - m_new)
        l_ref[...] = alpha * l_ref[...] + jnp.sum(p, axis=0, keepdims=True)
        m_ref[...] = m_new
        alpha_ref[slot] = alpha
        p_ref[slot] = p.astype(BF16)

    def step(kc, slot, masked, last):
        if not last:
            s_ref[1 - slot] = scores(kc + 1)
        apply_values(jnp.maximum(kc - 1, 0), 1 - slot)
        softmax(kc, slot, masked)

    s_ref[0] = scores(0)

    def body(j, carry):
        step(2 * j, 0, False, False)
        step(2 * j + 1, 1, False, False)
        return carry

    lax.fori_loop(0, n_pairs, body, 0)
    step(2 * n_pairs, 0, True, False)
    step(2 * n_pairs + 1, 1, True, True)
    apply_values(2 * n_pairs + 1, 1)

    lam = _diff_lambda(lq1_ref, lk1_ref, lq2_ref, lk2_ref, lambda_init)
    on = acc_ref[...] / l_ref[...]
    ot = on[:, :tq] - lam * on[:, tq:]
    o_ref[...] = _subln(ot.T, g_ref[...], lambda_init).astype(o_ref.dtype)


def _attn_prompt(qt, k, vt, lq1, lk1, lq2, lk2, g, lambda_init, tq, tk):
    B, S, _ = k.shape
    assert S % tq == 0 and tq == 2 * tk and tq & (tq - 1) == 0
    nq = S // tq
    lvec = _const_spec((1, HEAD_DIM), 3)
    return pl.pallas_call(
        functools.partial(_attn_p_kernel, tq=tq, tk=tk, lambda_init=lambda_init),
        grid=(B, N_HEADS, nq),
        in_specs=[
            pl.BlockSpec((V_DIM, tq), lambda b, h, i: (h, b * nq + i)),
            pl.BlockSpec((None, S, V_DIM), lambda b, h, i: (b, 0, h)),
            pl.BlockSpec((V_DIM, S), lambda b, h, i: (h, b)),
            lvec, lvec, lvec, lvec,
            _const_spec((1, V_DIM), 3),
        ],
        out_specs=pl.BlockSpec((None, tq, V_DIM), lambda b, h, i: (b, i, h)),
        out_shape=jax.ShapeDtypeStruct((B, S, D_MODEL), BF16),
        scratch_shapes=[
            pltpu.VMEM((V_DIM, 2 * tq), BF16),
            pltpu.VMEM((2, tk, 2 * tq), F32),
            pltpu.VMEM((2, tk, 2 * tq), BF16),
            pltpu.VMEM((2, 1, 2 * tq), F32),
            pltpu.VMEM((1, 2 * tq), F32),
            pltpu.VMEM((1, 2 * tq), F32),
            pltpu.VMEM((V_DIM, 2 * tq), F32),
        ],
        compiler_params=_params(("parallel", "parallel", "arbitrary")),
        name="attn_prompt",
    )(qt, k, vt, lq1, lk1, lq2, lk2, g)


def _attn_s_kernel(pt_ref, q_ref, kn_ref, vn_ref, lq1_ref, lk1_ref, lq2_ref, lk2_ref, g_ref, *rest,
                   pp, T, lambda_init):
    k_refs = rest[:pp]
    v_refs = rest[pp:2 * pp]
    o_ref, qm_ref, m_ref, l_ref, acc_ref = rest[2 * pp:]
    del pt_ref
    p = pl.program_id(1)
    n_rows = N_HEADS * 2 * T
    page_rows = PAGE_SIZE * N_HEADS

    @pl.when(p == 0)
    def _():
        q = q_ref[...].astype(F32)
        lane = lax.broadcasted_iota(jnp.int32, (T, V_DIM), 1)
        blocks = []
        for h in range(N_HEADS):
            qh = q[:, h * V_DIM:(h + 1) * V_DIM]
            blocks.append(jnp.where(lane < HEAD_DIM, qh, 0.0))
            blocks.append(jnp.where(lane >= HEAD_DIM, qh, 0.0))
        qm_ref[...] = jnp.concatenate(blocks, axis=0).astype(BF16)
        m_ref[...] = jnp.full(m_ref.shape, NEG_INF, F32)
        l_ref[...] = jnp.zeros(l_ref.shape, F32)
        acc_ref[...] = jnp.zeros(acc_ref.shape, F32)

    qm = qm_ref[...]

    def scores(kb):
        return lax.dot_general(qm, kb, _NT, preferred_element_type=F32)

    def fold(s, v_list):
        width = s.shape[1] // len(v_list)

        def pv(pb):
            out = None
            for i, vb in enumerate(v_list):
                part = jnp.dot(pb[:, i * width:(i + 1) * width], vb, preferred_element_type=F32)
                out = part if out is None else out + part
            return out
        _online_update(0, s, pv, m_ref, l_ref, acc_ref, axis=-1)

    rr = lax.broadcasted_iota(jnp.int32, (n_rows, page_rows), 0)
    cc = lax.broadcasted_iota(jnp.int32, (n_rows, page_rows), 1)
    same_head = (cc % N_HEADS) == (rr // (2 * T))
    s_past = jnp.concatenate(
        [jnp.where(same_head, scores(k_refs[i][...].astype(BF16)), NEG_INF) for i in range(pp)], axis=1)
    fold(s_past, [v_refs[i][...].astype(BF16) for i in range(pp)])

    @pl.when(p == pl.num_programs(1) - 1)
    def _():
        new_rows = T * N_HEADS
        pad = jnp.zeros((LANES - new_rows, V_DIM), F32)
        kn = jnp.concatenate([kn_ref[...], pad], axis=0).astype(BF16)
        vn = jnp.concatenate([vn_ref[...], pad], axis=0).astype(BF16)
        s = scores(kn)
        r2 = lax.broadcasted_iota(jnp.int32, s.shape, 0)
        c2 = lax.broadcasted_iota(jnp.int32, s.shape, 1)
        s = jnp.where((c2 % N_HEADS) == (r2 // (2 * T)), s, NEG_INF)
        s = jnp.where(c2 // N_HEADS <= r2 % T, s, NEG_INF)
        fold(s, [vn])

        lam = _diff_lambda(lq1_ref, lk1_ref, lq2_ref, lk2_ref, lambda_init)
        accn = acc_ref[0] / l_ref[0]
        for h in range(N_HEADS):
            r0 = h * 2 * T
            dh = accn[r0:r0 + T] - lam * accn[r0 + T:r0 + 2 * T]
            o_ref[:, h * V_DIM:(h + 1) * V_DIM] = _subln(dh, g_ref[...], lambda_init).astype(o_ref.dtype)


def _attn_sample(q, k_new, v_new, cache_k, cache_v, layer, page_table, lq1, lk1, lq2, lk2, g,
                 lambda_init, pp):
    DB, T, _ = q.shape
    n_pages = page_table.shape[1]
    assert n_pages % pp == 0 and T == SUBLANES and T * N_HEADS <= LANES
    n_rows = N_HEADS * 2 * T
    pt_flat = page_table.reshape(-1)
    seq = pl.BlockSpec((None, T, D_MODEL), lambda b, p, pt: (b, 0, 0))
    new = pl.BlockSpec((None, T * N_HEADS, V_DIM), lambda b, p, pt: (b, 0, 0))
    lvec = pl.BlockSpec((1, HEAD_DIM), lambda b, p, pt: (0, 0))

    def page_spec(i):
        return pl.BlockSpec((None, None, PAGE_SIZE * N_HEADS, V_DIM),
                            lambda b, p, pt: (layer, pt[b * n_pages + p * pp + i], 0, 0))

    grid_spec = pltpu.PrefetchScalarGridSpec(
        num_scalar_prefetch=1,
        grid=(DB, n_pages // pp),
        in_specs=[seq, new, new, lvec, lvec, lvec, lvec,
                  pl.BlockSpec((1, V_DIM), lambda b, p, pt: (0, 0))]
                 + [page_spec(i) for i in range(pp)] * 2,
        out_specs=seq,
        scratch_shapes=[
            pltpu.VMEM((n_rows, V_DIM), BF16),
            pltpu.VMEM((1, n_rows, 1), F32),
            pltpu.VMEM((1, n_rows, 1), F32),
            pltpu.VMEM((1, n_rows, V_DIM), F32),
        ],
    )
    return pl.pallas_call(
        functools.partial(_attn_s_kernel, pp=pp, T=T, lambda_init=lambda_init),
        grid_spec=grid_spec,
        out_shape=jax.ShapeDtypeStruct((DB, T, D_MODEL), BF16),
        compiler_params=_params(("parallel", "arbitrary")),
        name="attn_sample",
    )(pt_flat, q, k_new, v_new, lq1, lk1, lq2, lk2, g, *([cache_k] * pp), *([cache_v] * pp))


def _tail_kernel(x_ref, orn_ref, oat_ref, g1_ref, wgr_ref, wga_ref, wbr_ref, wba_ref, wo_ref,
                 g2_ref, wu_ref, wd_ref, gf_ref, o_ref, *, final):
    x = x_ref[...]
    xn = _rmsnorm_bf16(x, g1_ref[...])
    y = jax.nn.sigmoid(jnp.dot(xn, wgr_ref[...], preferred_element_type=F32)) * \
        jnp.dot(orn_ref[...], wbr_ref[...], preferred_element_type=F32)
    y = y + jax.nn.sigmoid(jnp.dot(xn, wga_ref[...], preferred_element_type=F32)) * \
        jnp.dot(oat_ref[...], wba_ref[...], preferred_element_type=F32)
    acc = x + jnp.dot(y.astype(BF16), wo_ref[...], preferred_element_type=F32)
    xn2 = _rmsnorm_bf16(acc, g2_ref[...])
    for c in range(D_FF // D_MODEL):
        cols = slice(c * D_MODEL, (c + 1) * D_MODEL)
        hid = jnp.maximum(jnp.dot(xn2, wu_ref[:, cols], preferred_element_type=F32), 0.0)
        acc = acc + jnp.dot((hid * hid).astype(BF16), wd_ref[cols, :], preferred_element_type=F32)
    if final:
        ms = jnp.mean(acc * acc, axis=-1, keepdims=True)
        acc = acc * lax.rsqrt(ms + NORM_EPS) * gf_ref[...]
    o_ref[...] = acc


def _tail(x, o_rnn, o_attn, g1, wgr, wga, wbr, wba, wo, g2, wu, wd, gf, final, tm):
    n = x.shape[0]
    row = pl.BlockSpec((tm, D_MODEL), lambda i: (i, 0))
    vec = _const_spec((1, D_MODEL), 1)
    w = _const_spec((D_MODEL, D_MODEL), 1)
    return pl.pallas_call(
        functools.partial(_tail_kernel, final=final),
        grid=(n // tm,),
        in_specs=[row, row, row, vec, w, w, w, w, w, vec,
                  _const_spec((D_MODEL, D_FF), 1), _const_spec((D_FF, D_MODEL), 1), vec],
        out_specs=row,
        out_shape=jax.ShapeDtypeStruct((n, D_MODEL), F32),
        compiler_params=_params(("parallel",)),
        name="tail",
    )(x, o_rnn, o_attn, g1, wgr, wga, wbr, wba, wo, g2, wu, wd, gf)


def _rope_tables(pos):
    half = HEAD_DIM // 2
    inv = ROPE_THETA ** (-2.0 * jnp.arange(half, dtype=F32) / HEAD_DIM)
    ang = pos.astype(F32)[:, None] * inv[None, :]
    cos = jnp.cos(ang)
    sin = jnp.sin(ang)
    reps = LANES // HEAD_DIM
    return (jnp.concatenate([cos, cos] * reps, axis=1),
            jnp.concatenate([-sin, sin] * reps, axis=1))


def _gate_tiles(w):
    per = MXU_DIM // RG_BLOCK
    w5 = w.reshape(RG_GROUPS, per, RG_BLOCK, RG_BLOCK)
    eye = jnp.eye(per, dtype=w.dtype)
    t = w5[:, :, :, None, :] * eye[None, :, None, :, None]
    return t.reshape(RG_GROUPS, MXU_DIM, MXU_DIM).astype(BF16)


def kernel(x_prompt, x_sample, cache_k, cache_v, state_h, state_conv, page_table, ln1_g, w_in, conv_w, conv_b, rg_wa, rg_ba, rg_wx, rg_bx, rg_lambda, lam_q1, lam_k1, lam_q2, lam_k2, subln_g, w_branch, w_out, ln2_g, w_up, w_down, final_g):
    B, S, _ = x_prompt.shape
    DB, T, _ = x_sample.shape
    depth = w_in.shape[0]
    n_pool = cache_k.shape[1]
    past_len = page_table.shape[1] * PAGE_SIZE

    cos_p, sin_p = _rope_tables(jnp.arange(S))
    cos_s, sin_s = _rope_tables(past_len + jnp.arange(DB * T) % T)
    ck = cache_k.reshape(depth, n_pool, PAGE_SIZE * N_HEADS, V_DIM)
    cv = cache_v.reshape(depth, n_pool, PAGE_SIZE * N_HEADS, V_DIM)

    xp = x_prompt.reshape(B * S, D_MODEL)
    xs = x_sample.reshape(DB * T, D_MODEL)
    h0_p = jnp.zeros((B, 1, D_MODEL), F32)
    c0_p = jnp.zeros((B, CONV_W - 1, D_MODEL), F32)
    gf = final_g.reshape(1, D_MODEL)

    outs = [[] for _ in range(8)]
    for l in range(depth):
        lambda_init = 0.8 - 0.6 * math.exp(-0.3 * l)
        w_in_bf = w_in[l].astype(BF16)
        seg = lambda s: w_in_bf[:, s * D_MODEL:(s + 1) * D_MODEL]
        wbr = w_branch[l, :D_MODEL].astype(BF16)
        wba = w_branch[l, D_MODEL:].astype(BF16)
        wo = w_out[l].astype(BF16)
        wu = w_up[l].astype(BF16)
        wd = w_down[l].astype(BF16)
        wa4 = _gate_tiles(rg_wa[l])
        wx4 = _gate_tiles(rg_wx[l])
        vec = lambda a: a[l].reshape(1, -1)
        lam_vecs = (vec(lam_q1), vec(lam_k1), vec(lam_q2), vec(lam_k2))
        final = l == depth - 1

        def layer(x, n_seq, seq_len, cos_t, sin_t, h0, c0, tm, t_rnn, prompt):
            qkv = _qkv(x, vec(ln1_g), seg(SEG_Q), seg(SEG_K), seg(SEG_V), cos_t, sin_t, tm, prompt)
            ko, vo = qkv[-2:]
            o_rnn, h_last, c_last = _rnn(x.reshape(n_seq, seq_len, D_MODEL), vec(ln1_g), seg(SEG_XR),
                                         seg(SEG_GR), h0, c0, conv_w[l], vec(conv_b), wa4, vec(rg_ba), wx4,
                                         vec(rg_bx), vec(rg_lambda), t_rnn)
            if prompt:
                qt, ka, vt = qkv[:3]
                o_attn = _attn_prompt(qt, ka.reshape(n_seq, seq_len, D_MODEL), vt, *lam_vecs, vec(subln_g),
                                      lambda_init, 512, 256)
            else:
                new_rows = lambda a: a.reshape(n_seq, seq_len * N_HEADS, V_DIM)
                o_attn = _attn_sample(qkv[0].reshape(n_seq, seq_len, D_MODEL), new_rows(ko), new_rows(vo),
                                      ck, cv, l, page_table, *lam_vecs, vec(subln_g), lambda_init, 8)
            x2 = _tail(x, o_rnn.reshape(-1, D_MODEL), o_attn.reshape(-1, D_MODEL), vec(ln1_g),
                       seg(SEG_GRNN), seg(SEG_GATTN), wbr, wba, wo, vec(ln2_g), wu, wd, gf, final, tm)
            as_cache = lambda a: a.reshape(n_seq, seq_len, N_HEADS, V_DIM)
            return x2, as_cache(ko), as_cache(vo), h_last.reshape(n_seq, D_MODEL), c_last

        xp, kp, vp, hp, cp = layer(xp, B, S, cos_p, sin_p, h0_p, c0_p, 512, 256, True)
        xs, ks, vs, hs, cs = layer(xs, DB, T, cos_s, sin_s, state_h[l].reshape(DB, 1, D_MODEL),
                                   state_conv[l], DB * T, T, False)
        for lst, val in zip(outs, (kp, vp, hp, cp, ks, vs, hs, cs)):
            lst.append(val)

    return (xp.reshape(B, S, D_MODEL), xs.reshape(DB, T, D_MODEL)) + tuple(jnp.stack(o) for o in outs)
```

```python
import functools
import math

import jax
import jax.numpy as jnp
from jax import lax
from jax.experimental import pallas as pl
from jax.experimental.pallas import tpu as pltpu

F32 = jnp.float32
BF16 = jnp.bfloat16

D_MODEL = 1024
N_HEADS = 8
HEAD_DIM = 64
V_DIM = 2 * HEAD_DIM
PAGE_SIZE = 128
ROPE_THETA = 10000.0
SUBLN_EPS = 1e-5
N_RG_BLOCKS = 16
RG_BLOCK = D_MODEL // N_RG_BLOCKS
CONV_W = 4
RG_C = 8.0
D_FF = 4 * D_MODEL
NORM_EPS = 1e-6
ATTN_SCALE = HEAD_DIM ** -0.5
LOG2_E = math.log2(math.e)
SEG_XR, SEG_GR, SEG_Q, SEG_K, SEG_V, SEG_GRNN, SEG_GATTN = range(7)

LANES = 128
SUBLANES = 8
MXU_DIM = 256
RG_GROUPS = D_MODEL // MXU_DIM
VMEM_LIMIT = 58 * 1024 * 1024

NEG_INF = float("-inf")
_NT = (((1,), (1,)), ((), ()))


def _params(semantics):
    return pltpu.CompilerParams(dimension_semantics=semantics, vmem_limit_bytes=VMEM_LIMIT)


def _const_spec(shape):
    zeros = (0,) * len(shape)
    return pl.BlockSpec(shape, lambda *_: zeros, pipeline_mode=pl.Buffered(1))


def _rmsnorm_bf16(x, g):
    ms = jnp.mean(x * x, axis=-1, keepdims=True)
    return (x * lax.rsqrt(ms + NORM_EPS) * g).astype(BF16)


def _diff_lambda(lq1_ref, lk1_ref, lq2_ref, lk2_ref, lambda_init):
    s1 = jnp.sum(lq1_ref[...] * lk1_ref[...], axis=-1, keepdims=True)
    s2 = jnp.sum(lq2_ref[...] * lk2_ref[...], axis=-1, keepdims=True)
    return jnp.exp(s1) - jnp.exp(s2) + lambda_init


def _subln(o, g, lambda_init):
    ms = jnp.mean(o * o, axis=-1, keepdims=True)
    return (o * lax.rsqrt(ms + SUBLN_EPS) * g) * (1.0 - lambda_init)


def _qkv_kernel(x_ref, g_ref, wq_ref, wk_ref, wv_ref, cos_ref, sin_ref, *refs, tm, transposed, n_aliased):
    out_refs = refs[n_aliased:]
    if transposed:
        q_ref, ka_ref, va_ref, ko_ref, vo_ref = out_refs
    else:
        q_ref, ko_ref, vo_ref = out_refs
    if n_aliased == 0:
        for ref in (ko_ref, vo_ref):
            if ref.shape[0] > 1:
                ref[1:] = jnp.zeros((ref.shape[0] - 1,) + ref.shape[1:], F32)
        ko_ref, vo_ref = ko_ref.at[0], vo_ref.at[0]
    xn = _rmsnorm_bf16(x_ref[...], g_ref[...])
    cos = cos_ref[...]
    sin = sin_ref[...]
    lane = lax.broadcasted_iota(jnp.int32, cos.shape, 1)
    lower = (lane & (HEAD_DIM // 2)) == 0

    def head_cols(h):
        return slice(h * V_DIM, (h + 1) * V_DIM)

    def rope_heads(z):
        for h in range(N_HEADS):
            zh = z[:, head_cols(h)]
            partner = jnp.where(lower,
                                pltpu.roll(zh, LANES - HEAD_DIM // 2, 1),
                                pltpu.roll(zh, HEAD_DIM // 2, 1))
            yield h, zh * cos + partner * sin

    zq = jnp.dot(xn, wq_ref[...], preferred_element_type=F32)
    for h, qh in rope_heads(zq):
        qh = qh * (ATTN_SCALE * LOG2_E)
        if transposed:
            q_ref[head_cols(h), :] = qh.T.astype(BF16)
        else:
            q_ref[:, head_cols(h)] = qh.astype(BF16)

    zk = jnp.dot(xn, wk_ref[...], preferred_element_type=F32)
    for h, kh in rope_heads(zk):
        ko_ref[pl.ds(h, tm, stride=N_HEADS), :] = kh
        if transposed:
            ka_ref[:, head_cols(h)] = kh.astype(BF16)

    zv = jnp.dot(xn, wv_ref[...], preferred_element_type=F32)
    for h in range(N_HEADS):
        vh = zv[:, head_cols(h)]
        vo_ref[pl.ds(h, tm, stride=N_HEADS), :] = vh
        if transposed:
            va_ref[head_cols(h), :] = vh.T.astype(BF16)


def _qkv(x, g, wq, wk, wv, cos_t, sin_t, tm, transposed, layer, depth, earlier):
    n = x.shape[0]
    n_tiles = n // tm
    n_tab = cos_t.shape[0] // tm
    row = lambda i: (i, 0)
    flat = pl.BlockSpec((tm, D_MODEL), row)
    chan = pl.BlockSpec((D_MODEL, tm), lambda i: (0, i))
    if earlier is None:
        assert layer == 0
        cache = pl.BlockSpec((depth, tm * N_HEADS, V_DIM), lambda i: (0, i, 0))
    else:
        cache = pl.BlockSpec((None, tm * N_HEADS, V_DIM), lambda i: (layer, i, 0))
    tab = pl.BlockSpec((tm, LANES), lambda i: (i % n_tab, 0))
    w = _const_spec((D_MODEL, D_MODEL))
    flat_bf16 = jax.ShapeDtypeStruct((n, D_MODEL), BF16)
    chan_bf16 = jax.ShapeDtypeStruct((D_MODEL, n), BF16)
    cache_f32 = jax.ShapeDtypeStruct((depth, n * N_HEADS, V_DIM), F32)
    if transposed:
        out_specs = [chan, flat, chan, cache, cache]
        out_shape = [chan_bf16, flat_bf16, chan_bf16, cache_f32, cache_f32]
    else:
        out_specs = [flat, cache, cache]
        out_shape = [flat_bf16, cache_f32, cache_f32]
    in_specs = [flat, _const_spec((1, D_MODEL)), w, w, w, tab, tab]
    args = [x, g, wq, wk, wv, cos_t, sin_t]
    aliases = {}
    if earlier is not None:
        n_out = len(out_shape)
        aliases = {len(args): n_out - 2, len(args) + 1: n_out - 1}
        in_specs += [pl.BlockSpec(memory_space=pl.ANY)] * 2
        args += list(earlier)
    return pl.pallas_call(
        functools.partial(_qkv_kernel, tm=tm, transposed=transposed, n_aliased=len(aliases)),
        grid=(n_tiles,),
        in_specs=in_specs,
        out_specs=out_specs,
        out_shape=out_shape,
        input_output_aliases=aliases,
        compiler_params=_params(("parallel",)),
        name="qkv",
    )(*args)


def _rnn_kernel(x_ref, g_ref, wxr_ref, wgr_ref, h0_ref, c0_ref, cw_ref, cb_ref,
                wa_ref, ba_ref, wx_ref, bx_ref, lam_ref,
                o_ref, hl_ref, cl_ref,
                xe_ref, a_ref, u_ref, gg_ref, al_ref, bl_ref, o32_ref, h_ref, *, T):
    t = pl.program_id(1)
    halo = CONV_W - 1
    G = T // SUBLANES
    NB = D_MODEL // LANES

    @pl.when(t == 0)
    def _():
        xe_ref[SUBLANES - halo:SUBLANES, :] = c0_ref[...]
        h_ref[...] = h0_ref[...]

    @pl.when(t > 0)
    def _():
        xe_ref[SUBLANES - halo:SUBLANES, :] = xe_ref[T + SUBLANES - halo:T + SUBLANES, :]

    def to_blocks(ref, val):
        for c in range(NB):
            ref[c] = val[:, c * LANES:(c + 1) * LANES]

    xn = _rmsnorm_bf16(x_ref[...], g_ref[...])
    xe_ref[SUBLANES:SUBLANES + T, :] = jnp.dot(xn, wxr_ref[...], preferred_element_type=F32)
    to_blocks(gg_ref, jax.nn.gelu(jnp.dot(xn, wgr_ref[...], preferred_element_type=F32)))

    xc = cb_ref[...]
    for j in range(CONV_W):
        s = SUBLANES - halo + j
        xc = xc + xe_ref[s:s + T, :] * cw_ref[j:j + 1, :]

    xcb = xc.astype(BF16)

    def gate(w_ref, bias_ref):
        parts = [jnp.dot(xcb[:, g * MXU_DIM:(g + 1) * MXU_DIM], w_ref[g], preferred_element_type=F32)
                 for g in range(RG_GROUPS)]
        return jax.nn.sigmoid(jnp.concatenate(parts, axis=1) + bias_ref[...])

    r = gate(wa_ref, ba_ref)
    i = gate(wx_ref, bx_ref)
    log_a = r * ((-RG_C) * jax.nn.softplus(-lam_ref[...]))
    a = jnp.exp(log_a)
    to_blocks(a_ref, a)
    to_blocks(u_ref, jnp.sqrt(1.0 - a * a) * (i * xc))

    def rows(ref, c, k):
        return ref[c, pl.ds(k, G, stride=SUBLANES), :]

    for c in range(NB):
        cols = slice(c * LANES, (c + 1) * LANES)
        a_cum = rows(a_ref, c, 0)
        b_cum = rows(u_ref, c, 0)
        al_ref[0, c] = a_cum
        bl_ref[0, c] = b_cum
        for k in range(1, SUBLANES):
            ak = rows(a_ref, c, k)
            b_cum = ak * b_cum + rows(u_ref, c, k)
            a_cum = ak * a_cum
            al_ref[k, c] = a_cum
            bl_ref[k, c] = b_cum
        h = h_ref[:, cols]
        for gi in range(G):
            a_ref[c, gi:gi + 1, :] = h
            h = bl_ref[SUBLANES - 1, c, gi:gi + 1, :] + al_ref[SUBLANES - 1, c, gi:gi + 1, :] * h
        h_ref[:, cols] = h
        h_in = a_ref[c, 0:G, :]
        for k in range(SUBLANES):
            hk = bl_ref[k, c] + al_ref[k, c] * h_in
            o32_ref[c, pl.ds(k, G, stride=SUBLANES), :] = hk * rows(gg_ref, c, k)
        o_ref[:, cols] = o32_ref[c].astype(o_ref.dtype)
    hl_ref[...] = h_ref[...]
    cl_ref[...] = xe_ref[T + SUBLANES - halo:T + SUBLANES, :]


def _rnn(x, g, wxr, wgr, h0, c0, cw, cb, wa4, ba, wx4, bx, lam, T):
    B, S, _ = x.shape
    assert S % T == 0 and T % SUBLANES == 0 and T >= CONV_W - 1
    G = T // SUBLANES
    NB = D_MODEL // LANES
    seq = lambda b, t: (b, t, 0)
    per_b = lambda b, t: (b, 0, 0)
    vec = _const_spec((1, D_MODEL))
    w = _const_spec((D_MODEL, D_MODEL))
    wgate = _const_spec((RG_GROUPS, MXU_DIM, MXU_DIM))
    tile = pltpu.VMEM((NB, T, LANES), F32)
    return pl.pallas_call(
        functools.partial(_rnn_kernel, T=T),
        grid=(B, S // T),
        in_specs=[
            pl.BlockSpec((None, T, D_MODEL), seq),
            vec, w, w,
            pl.BlockSpec((None, 1, D_MODEL), per_b),
            pl.BlockSpec((None, CONV_W - 1, D_MODEL), per_b),
            _const_spec((CONV_W, D_MODEL)),
            vec, wgate, vec, wgate, vec, vec,
        ],
        out_specs=[
            pl.BlockSpec((None, T, D_MODEL), seq),
            pl.BlockSpec((None, 1, D_MODEL), per_b),
            pl.BlockSpec((None, CONV_W - 1, D_MODEL), per_b),
        ],
        out_shape=[
            jax.ShapeDtypeStruct((B, S, D_MODEL), BF16),
            jax.ShapeDtypeStruct((B, 1, D_MODEL), F32),
            jax.ShapeDtypeStruct((B, CONV_W - 1, D_MODEL), F32),
        ],
        scratch_shapes=[
            pltpu.VMEM((T + SUBLANES, D_MODEL), F32),
            tile, tile, tile,
            pltpu.VMEM((SUBLANES, NB, G, LANES), F32),
            pltpu.VMEM((SUBLANES, NB, G, LANES), F32),
            tile,
            pltpu.VMEM((1, D_MODEL), F32),
        ],
        compiler_params=_params(("parallel", "arbitrary")),
        name="rnn",
    )(x, g, wxr, wgr, h0, c0, cw, cb, wa4, ba, wx4, bx, lam)


def _attn_p_kernel(qt_ref, k_ref, vt_ref, lq1_ref, lk1_ref, lq2_ref, lk2_ref, g_ref,
                   o_ref, q2_ref, s_ref, mx_ref, p_ref, alpha_ref, m_ref, l_ref, acc_ref,
                   *, tq, tk, hps, lambda_init):
    qi = pl.program_id(2)
    n_pairs = qi
    heads = range(hps)

    def head_rows(h):
        return slice(h * V_DIM, (h + 1) * V_DIM)

    row = lax.broadcasted_iota(jnp.int32, (V_DIM, tq), 0)
    for h in heads:
        qt = qt_ref[head_rows(h), :]
        zero = jnp.zeros_like(qt)
        q2_ref[h, :, :tq] = jnp.where(row < HEAD_DIM, qt, zero)
        q2_ref[h, :, tq:] = jnp.where(row >= HEAD_DIM, qt, zero)
    m_ref[...] = jnp.full(m_ref.shape, NEG_INF, F32)
    l_ref[...] = jnp.zeros(l_ref.shape, F32)
    acc_ref[...] = jnp.zeros(acc_ref.shape, F32)
    for h in heads:
        p_ref[h, 1] = jnp.zeros(p_ref.shape[2:], BF16)
        alpha_ref[h, 1] = jnp.ones(alpha_ref.shape[2:], F32)

    def produce(h, kc, slot, masked):
        r0 = pl.multiple_of(kc * tk, tk)
        s = jnp.dot(k_ref[pl.ds(r0, tk), head_rows(h)], q2_ref[h], preferred_element_type=F32)
        if masked:
            key = kc * tk + lax.broadcasted_iota(jnp.int32, s.shape, 0)
            qry = qi * tq + (lax.broadcasted_iota(jnp.int32, s.shape, 1) & (tq - 1))
            s = jnp.where(key <= qry, s, NEG_INF)
        s_ref[h, slot] = s
        mx_ref[h, slot] = jnp.max(s, axis=0, keepdims=True)

    def apply_values(h, kc, slot):
        r0 = pl.multiple_of(kc * tk, tk)
        pv = jnp.dot(vt_ref[head_rows(h), pl.ds(r0, tk)], p_ref[h, slot], preferred_element_type=F32)
        acc_ref[h] = alpha_ref[h, slot] * acc_ref[h] + pv

    def softmax(h, slot):
        m_old = m_ref[h]
        m_new = jnp.maximum(m_old, mx_ref[h, slot])
        alpha = jnp.exp2(m_old - m_new)
        p = jnp.exp2(s_ref[h, slot] - m_new)
        l_ref[h] = alpha * l_ref[h] + jnp.sum(p, axis=0, keepdims=True)
        m_ref[h] = m_new
        alpha_ref[h, slot] = alpha
        p_ref[h, slot] = p.astype(BF16)

    def step(kc, slot, next_masked):
        if next_masked is not None:
            for h in heads:
                produce(h, kc + 1, 1 - slot, next_masked)
        for h in heads:
            apply_values(h, jnp.maximum(kc - 1, 0), 1 - slot)
        for h in heads:
            softmax(h, slot)

    def pair(j, then_masked):
        step(2 * j, 0, False)
        step(2 * j + 1, 1, then_masked)

    for h in heads:
        produce(h, 0, 0, True)

    def body(j, carry):
        pair(j, False)
        return carry

    lax.fori_loop(0, n_pairs - 1, body, 0)

    @pl.when(n_pairs > 0)
    def _():
        pair(n_pairs - 1, True)

    step(2 * n_pairs, 0, True)
    step(2 * n_pairs + 1, 1, None)
    lam = _diff_lambda(lq1_ref, lk1_ref, lq2_ref, lk2_ref, lambda_init)
    for h in heads:
        apply_values(h, 2 * n_pairs + 1, 1)
        on = acc_ref[h] / l_ref[h]
        ot = on[:, :tq] - lam * on[:, tq:]
        o_ref[:, head_rows(h)] = _subln(ot.T, g_ref[...], lambda_init).astype(o_ref.dtype)


def _attn_prompt(qt, k, vt, lq1, lk1, lq2, lk2, g, lambda_init, tq, tk, hps):
    B, S, _ = k.shape
    assert S % tq == 0 and tq == 2 * tk and tq & (tq - 1) == 0 and N_HEADS % hps == 0
    nq = S // tq
    lvec = _const_spec((1, HEAD_DIM))
    slots = lambda shape, dtype: pltpu.VMEM((hps, 2) + shape, dtype)
    return pl.pallas_call(
        functools.partial(_attn_p_kernel, tq=tq, tk=tk, hps=hps, lambda_init=lambda_init),
        grid=(B, N_HEADS // hps, nq),
        in_specs=[
            pl.BlockSpec((hps * V_DIM, tq), lambda b, h, i: (h, b * nq + i)),
            pl.BlockSpec((None, S, hps * V_DIM), lambda b, h, i: (b, 0, h)),
            pl.BlockSpec((hps * V_DIM, S), lambda b, h, i: (h, b)),
            lvec, lvec, lvec, lvec,
            _const_spec((1, V_DIM)),
        ],
        out_specs=pl.BlockSpec((None, tq, hps * V_DIM), lambda b, h, i: (b, i, h)),
        out_shape=jax.ShapeDtypeStruct((B, S, D_MODEL), BF16),
        scratch_shapes=[
            pltpu.VMEM((hps, V_DIM, 2 * tq), BF16),
            slots((tk, 2 * tq), F32),
            slots((1, 2 * tq), F32),
            slots((tk, 2 * tq), BF16),
            slots((1, 2 * tq), F32),
            pltpu.VMEM((hps, 1, 2 * tq), F32),
            pltpu.VMEM((hps, 1, 2 * tq), F32),
            pltpu.VMEM((hps, V_DIM, 2 * tq), F32),
        ],
        compiler_params=_params(("parallel", "parallel", "arbitrary")),
        name="attn_prompt",
    )(qt, k, vt, lq1, lk1, lq2, lk2, g)


def _attn_s_kernel(pt_ref, q_ref, kn_ref, vn_ref, lq1_ref, lk1_ref, lq2_ref, lk2_ref, g_ref, *rest,
                   pp, T, lambda_init):
    k_refs = rest[:pp]
    v_refs = rest[pp:2 * pp]
    o_ref, qm_ref, bias_ref, m_ref, l_ref, acc_ref = rest[2 * pp:]
    del pt_ref
    p = pl.program_id(1)
    n_rows = N_HEADS * 2 * T
    page_rows = PAGE_SIZE * N_HEADS

    @pl.when(p == 0)
    def _():
        q = q_ref[...].astype(F32)
        lane = lax.broadcasted_iota(jnp.int32, (T, V_DIM), 1)
        blocks = []
        for h in range(N_HEADS):
            qh = q[:, h * V_DIM:(h + 1) * V_DIM]
            blocks.append(jnp.where(lane < HEAD_DIM, qh, 0.0))
            blocks.append(jnp.where(lane >= HEAD_DIM, qh, 0.0))
        qm_ref[...] = jnp.concatenate(blocks, axis=0).astype(BF16)
        rr = lax.broadcasted_iota(jnp.int32, bias_ref.shape, 0)
        cc = lax.broadcasted_iota(jnp.int32, bias_ref.shape, 1)
        bias_ref[...] = jnp.where((cc % N_HEADS) == (rr // (2 * T)), 0.0, NEG_INF)
        m_ref[...] = jnp.full(m_ref.shape, NEG_INF, F32)
        l_ref[...] = jnp.zeros(l_ref.shape, F32)
        acc_ref[...] = jnp.zeros(acc_ref.shape, F32)

    qm = qm_ref[...]

    def scores(kb):
        s = lax.dot_general(qm, kb, _NT, preferred_element_type=F32)
        return s + bias_ref[:, :s.shape[1]]

    def fold(s, v_list):
        width = s.shape[1] // len(v_list)
        m_old = m_ref[...]
        m_new = jnp.maximum(m_old, jnp.max(s, axis=-1, keepdims=True))
        alpha = jnp.exp2(m_old - m_new)
        pb = jnp.exp2(s - m_new)
        l_ref[...] = alpha * l_ref[...] + jnp.sum(pb, axis=-1, keepdims=True)
        pb = pb.astype(BF16)
        pv = None
        for i, vb in enumerate(v_list):
            part = jnp.dot(pb[:, i * width:(i + 1) * width], vb, preferred_element_type=F32)
            pv = part if pv is None else pv + part
        acc_ref[...] = alpha * acc_ref[...] + pv
        m_ref[...] = m_new

    s_pages = [scores(k_refs[i][...].astype(BF16)) for i in range(pp)]
    for a in range(0, pp, pp // 2):
        group = range(a, a + pp // 2)
        fold(jnp.concatenate([s_pages[i] for i in group], axis=1), [v_refs[i][...].astype(BF16) for i in group])

    @pl.when(p == pl.num_programs(1) - 1)
    def _():
        new_rows = T * N_HEADS
        pad = jnp.zeros((LANES - new_rows, V_DIM), F32)
        kn = jnp.concatenate([kn_ref[...], pad], axis=0).astype(BF16)
        vn = jnp.concatenate([vn_ref[...], pad], axis=0).astype(BF16)
        s = scores(kn)
        r2 = lax.broadcasted_iota(jnp.int32, s.shape, 0)
        c2 = lax.broadcasted_iota(jnp.int32, s.shape, 1)
        s = jnp.where(c2 // N_HEADS <= r2 % T, s, NEG_INF)
        fold(s, [vn])

        lam = _diff_lambda(lq1_ref, lk1_ref, lq2_ref, lk2_ref, lambda_init)
        accn = acc_ref[...] / l_ref[...]
        for h in range(N_HEADS):
            r0 = h * 2 * T
            dh = accn[r0:r0 + T] - lam * accn[r0 + T:r0 + 2 * T]
            o_ref[:, h * V_DIM:(h + 1) * V_DIM] = _subln(dh, g_ref[...], lambda_init).astype(o_ref.dtype)


def _attn_sample(q, k_new, v_new, cache_k, cache_v, layer, page_table, lq1, lk1, lq2, lk2, g,
                 lambda_init, pp):
    DB, T, _ = q.shape
    n_pages = page_table.shape[1]
    assert n_pages % pp == 0 and pp % 2 == 0 and T == SUBLANES and T * N_HEADS <= LANES
    n_rows = N_HEADS * 2 * T
    pt_flat = page_table.reshape(-1)
    seq = pl.BlockSpec((None, T, D_MODEL), lambda b, p, pt: (b, 0, 0))
    new = pl.BlockSpec((None, T * N_HEADS, V_DIM), lambda b, p, pt: (b, 0, 0))
    lvec = pl.BlockSpec((1, HEAD_DIM), lambda b, p, pt: (0, 0))

    def page_spec(i):
        return pl.BlockSpec((None, None, PAGE_SIZE * N_HEADS, V_DIM),
                            lambda b, p, pt: (layer, pt[b * n_pages + p * pp + i], 0, 0))

    grid_spec = pltpu.PrefetchScalarGridSpec(
        num_scalar_prefetch=1,
        grid=(DB, n_pages // pp),
        in_specs=[seq, new, new, lvec, lvec, lvec, lvec,
                  pl.BlockSpec((1, V_DIM), lambda b, p, pt: (0, 0))]
                 + [page_spec(i) for i in range(pp)] * 2,
        out_specs=seq,
        scratch_shapes=[
            pltpu.VMEM((n_rows, V_DIM), BF16),
            pltpu.VMEM((n_rows, PAGE_SIZE * N_HEADS), F32),
            pltpu.VMEM((n_rows, 1), F32),
            pltpu.VMEM((n_rows, 1), F32),
            pltpu.VMEM((n_rows, V_DIM), F32),
        ],
    )
    return pl.pallas_call(
        functools.partial(_attn_s_kernel, pp=pp, T=T, lambda_init=lambda_init),
        grid_spec=grid_spec,
        out_shape=jax.ShapeDtypeStruct((DB, T, D_MODEL), BF16),
        compiler_params=_params(("parallel", "arbitrary")),
        name="attn_sample",
    )(pt_flat, q, k_new, v_new, lq1, lk1, lq2, lk2, g, *([cache_k] * pp), *([cache_v] * pp))


def _tail_kernel(x_ref, orn_ref, oat_ref, g1_ref, wgr_ref, wga_ref, wbr_ref, wba_ref, wo_ref,
                 g2_ref, wu_ref, wd_ref, gf_ref, o_ref, *, final):
    x = x_ref[...]
    xn = _rmsnorm_bf16(x, g1_ref[...])
    y = jax.nn.sigmoid(jnp.dot(xn, wgr_ref[...], preferred_element_type=F32)) * \
        jnp.dot(orn_ref[...], wbr_ref[...], preferred_element_type=F32)
    y = y + jax.nn.sigmoid(jnp.dot(xn, wga_ref[...], preferred_element_type=F32)) * \
        jnp.dot(oat_ref[...], wba_ref[...], preferred_element_type=F32)
    acc = x + jnp.dot(y.astype(BF16), wo_ref[...], preferred_element_type=F32)
    xn2 = _rmsnorm_bf16(acc, g2_ref[...])
    for c in range(D_FF // D_MODEL):
        cols = slice(c * D_MODEL, (c + 1) * D_MODEL)
        hid = jnp.maximum(jnp.dot(xn2, wu_ref[:, cols], preferred_element_type=F32), 0.0)
        acc = acc + jnp.dot((hid * hid).astype(BF16), wd_ref[cols, :], preferred_element_type=F32)
    if final:
        ms = jnp.mean(acc * acc, axis=-1, keepdims=True)
        acc = acc * lax.rsqrt(ms + NORM_EPS) * gf_ref[...]
    o_ref[...] = acc


def _tail(x, o_rnn, o_attn, g1, wgr, wga, wbr, wba, wo, g2, wu, wd, gf, final, tm):
    n = x.shape[0]
    row = pl.BlockSpec((tm, D_MODEL), lambda i: (i, 0))
    vec = _const_spec((1, D_MODEL))
    w = _const_spec((D_MODEL, D_MODEL))
    return pl.pallas_call(
        functools.partial(_tail_kernel, final=final),
        grid=(n // tm,),
        in_specs=[row, row, row, vec, w, w, w, w, w, vec,
                  _const_spec((D_MODEL, D_FF)), _const_spec((D_FF, D_MODEL)), vec],
        out_specs=row,
        out_shape=jax.ShapeDtypeStruct((n, D_MODEL), F32),
        compiler_params=_params(("parallel",)),
        name="tail",
    )(x, o_rnn, o_attn, g1, wgr, wga, wbr, wba, wo, g2, wu, wd, gf)


def _rope_tables(pos):
    half = HEAD_DIM // 2
    inv = ROPE_THETA ** (-2.0 * jnp.arange(half, dtype=F32) / HEAD_DIM)
    ang = pos.astype(F32)[:, None] * inv[None, :]
    cos = jnp.cos(ang)
    sin = jnp.sin(ang)
    reps = LANES // HEAD_DIM
    return (jnp.concatenate([cos, cos] * reps, axis=1),
            jnp.concatenate([-sin, sin] * reps, axis=1))


def _gate_tiles(w):
    per = MXU_DIM // RG_BLOCK
    w5 = w.reshape(RG_GROUPS, per, RG_BLOCK, RG_BLOCK)
    eye = jnp.eye(per, dtype=w.dtype)
    t = w5[:, :, :, None, :] * eye[None, :, None, :, None]
    return t.reshape(RG_GROUPS, MXU_DIM, MXU_DIM).astype(BF16)


def kernel(x_prompt, x_sample, cache_k, cache_v, state_h, state_conv, page_table, ln1_g, w_in, conv_w, conv_b, rg_wa, rg_ba, rg_wx, rg_bx, rg_lambda, lam_q1, lam_k1, lam_q2, lam_k2, subln_g, w_branch, w_out, ln2_g, w_up, w_down, final_g):
    B, S, _ = x_prompt.shape
    DB, T, _ = x_sample.shape
    depth = w_in.shape[0]
    n_pool = cache_k.shape[1]
    past_len = page_table.shape[1] * PAGE_SIZE

    cos_p, sin_p = _rope_tables(jnp.arange(S))
    cos_s, sin_s = _rope_tables(past_len + jnp.arange(DB * T) % T)
    ck = cache_k.reshape(depth, n_pool, PAGE_SIZE * N_HEADS, V_DIM)
    cv = cache_v.reshape(depth, n_pool, PAGE_SIZE * N_HEADS, V_DIM)

    xp = x_prompt.reshape(B * S, D_MODEL)
    xs = x_sample.reshape(DB * T, D_MODEL)
    h0_p = jnp.zeros((B, 1, D_MODEL), F32)
    c0_p = jnp.zeros((B, CONV_W - 1, D_MODEL), F32)
    gf = final_g.reshape(1, D_MODEL)

    outs = [[] for _ in range(4)]
    kv_p = kv_s = None
    for l in range(depth):
        lambda_init = 0.8 - 0.6 * math.exp(-0.3 * l)
        w_in_bf = w_in[l].astype(BF16)
        seg = lambda s: w_in_bf[:, s * D_MODEL:(s + 1) * D_MODEL]
        wbr = w_branch[l, :D_MODEL].astype(BF16)
        wba = w_branch[l, D_MODEL:].astype(BF16)
        wo = w_out[l].astype(BF16)
        wu = w_up[l].astype(BF16)
        wd = w_down[l].astype(BF16)
        wa4 = _gate_tiles(rg_wa[l])
        wx4 = _gate_tiles(rg_wx[l])
        vec = lambda a: a[l].reshape(1, -1)
        lam_vecs = (vec(lam_q1), vec(lam_k1), vec(lam_q2), vec(lam_k2))
        final = l == depth - 1

        def layer(x, n_seq, seq_len, cos_t, sin_t, h0, c0, tm, t_rnn, prompt, kv_all):
            qkv = _qkv(x, vec(ln1_g), seg(SEG_Q), seg(SEG_K), seg(SEG_V), cos_t, sin_t, tm, prompt,
                       l, depth, kv_all)
            kv_all = qkv[-2:]
            o_rnn, h_last, c_last = _rnn(x.reshape(n_seq, seq_len, D_MODEL), vec(ln1_g), seg(SEG_XR),
                                         seg(SEG_GR), h0, c0, conv_w[l], vec(conv_b), wa4, vec(rg_ba), wx4,
                                         vec(rg_bx), vec(rg_lambda), t_rnn)
            if prompt:
                qt, ka, vt = qkv[:3]
                o_attn = _attn_prompt(qt, ka.reshape(n_seq, seq_len, D_MODEL), vt, *lam_vecs, vec(subln_g),
                                      lambda_init, 512, 256, 4)
            else:
                new_rows = lambda a: a[l].reshape(n_seq, seq_len * N_HEADS, V_DIM)
                o_attn = _attn_sample(qkv[0].reshape(n_seq, seq_len, D_MODEL), new_rows(kv_all[0]),
                                      new_rows(kv_all[1]), ck, cv, l, page_table, *lam_vecs, vec(subln_g),
                                      lambda_init, 8)
            x2 = _tail(x, o_rnn.reshape(-1, D_MODEL), o_attn.reshape(-1, D_MODEL), vec(ln1_g),
                       seg(SEG_GRNN), seg(SEG_GATTN), wbr, wba, wo, vec(ln2_g), wu, wd, gf, final, tm)
            return x2, kv_all, h_last.reshape(n_seq, D_MODEL), c_last

        xp, kv_p, hp, cp = layer(xp, B, S, cos_p, sin_p, h0_p, c0_p, 512, 256, True, kv_p)
        xs, kv_s, hs, cs = layer(xs, DB, T, cos_s, sin_s, state_h[l].reshape(DB, 1, D_MODEL),
                                 state_conv[l], DB * T, T, False, kv_s)
        for lst, val in zip(outs, (hp, cp, hs, cs)):
            lst.append(val)

    h_p, c_p, h_s, c_s = (jnp.stack(o) for o in outs)
    as_cache = lambda a, n_seq, seq_len: a.reshape(depth, n_seq, seq_len, N_HEADS, V_DIM)
    return (xp.reshape(B, S, D_MODEL), xs.reshape(DB, T, D_MODEL),
            as_cache(kv_p[0], B, S), as_cache(kv_p[1], B, S), h_p, c_p,
            as_cache(kv_s[0], DB, T), as_cache(kv_s[1], DB, T), h_s, c_s)
```

```python
import functools
import math

import jax
import jax.numpy as jnp
from jax import lax
from jax.experimental import pallas as pl
from jax.experimental.pallas import tpu as pltpu

F32 = jnp.float32
BF16 = jnp.bfloat16

D_MODEL = 1024
N_HEADS = 8
HEAD_DIM = 64
V_DIM = 2 * HEAD_DIM
PAGE_SIZE = 128
ROPE_THETA = 10000.0
SUBLN_EPS = 1e-5
N_RG_BLOCKS = 16
RG_BLOCK = D_MODEL // N_RG_BLOCKS
CONV_W = 4
RG_C = 8.0
D_FF = 4 * D_MODEL
NORM_EPS = 1e-6
ATTN_SCALE = HEAD_DIM ** -0.5
LOG2_E = math.log2(math.e)
SEG_XR, SEG_GR, SEG_Q, SEG_K, SEG_V, SEG_GRNN, SEG_GATTN = range(7)

LANES = 128
SUBLANES = 8
MXU_DIM = 256
RG_GROUPS = D_MODEL // MXU_DIM
VMEM_LIMIT = 58 * 1024 * 1024

ROW_TILE = 512
RNN_TILE = 256
ATTN_TQ, ATTN_TK = 512, 256
ATTN_HEADS_PER_STEP = 4
PAGES_PER_STEP = 16

NEG_INF = float("-inf")
_NT = (((1,), (1,)), ((), ()))


def _params(semantics):
    return pltpu.CompilerParams(dimension_semantics=semantics, vmem_limit_bytes=VMEM_LIMIT)


def _const_spec(shape, index=None):
    index = (0,) * len(shape) if index is None else index
    return pl.BlockSpec(shape, lambda *_: index, pipeline_mode=pl.Buffered(1))


def _in_segment_spec(segment):
    return _const_spec((D_MODEL, D_MODEL), (0, segment))


def _rmsnorm_bf16(x, g):
    ms = jnp.mean(x * x, axis=-1, keepdims=True)
    return (x * lax.rsqrt(ms + NORM_EPS) * g).astype(BF16)


def _diff_lambda(lq1_ref, lk1_ref, lq2_ref, lk2_ref, lambda_init):
    s1 = jnp.sum(lq1_ref[...] * lk1_ref[...], axis=-1, keepdims=True)
    s2 = jnp.sum(lq2_ref[...] * lk2_ref[...], axis=-1, keepdims=True)
    return jnp.exp(s1) - jnp.exp(s2) + lambda_init


def _subln(o, g, lambda_init):
    ms = jnp.mean(o * o, axis=-1, keepdims=True)
    return (o * lax.rsqrt(ms + SUBLN_EPS) * g) * (1.0 - lambda_init)


def _qkv_kernel(x_ref, g_ref, wq_ref, wk_ref, wv_ref, cos_ref, sin_ref, *refs, tm, transposed, n_aliased):
    out_refs = refs[n_aliased:]
    if transposed:
        q_ref, ka_ref, va_ref, ko_ref, vo_ref = out_refs
    else:
        q_ref, ko_ref, vo_ref = out_refs
    if n_aliased == 0:
        for ref in (ko_ref, vo_ref):
            if ref.shape[0] > 1:
                ref[1:] = jnp.zeros((ref.shape[0] - 1,) + ref.shape[1:], F32)
        ko_ref, vo_ref = ko_ref.at[0], vo_ref.at[0]
    xn = _rmsnorm_bf16(x_ref[...], g_ref[...])
    cos = cos_ref[...]
    sin = sin_ref[...]
    lane = lax.broadcasted_iota(jnp.int32, cos.shape, 1)
    lower = (lane & (HEAD_DIM // 2)) == 0

    def head_cols(h):
        return slice(h * V_DIM, (h + 1) * V_DIM)

    def rope_heads(z):
        for h in range(N_HEADS):
            zh = z[:, head_cols(h)]
            partner = jnp.where(lower,
                                pltpu.roll(zh, LANES - HEAD_DIM // 2, 1),
                                pltpu.roll(zh, HEAD_DIM // 2, 1))
            yield h, zh * cos + partner * sin

    zq = jnp.dot(xn, wq_ref[...], preferred_element_type=F32)
    for h, qh in rope_heads(zq):
        qh = qh * (ATTN_SCALE * LOG2_E)
        if transposed:
            q_ref[head_cols(h), :] = qh.T.astype(BF16)
        else:
            q_ref[:, head_cols(h)] = qh.astype(BF16)

    zk = jnp.dot(xn, wk_ref[...], preferred_element_type=F32)
    for h, kh in rope_heads(zk):
        ko_ref[pl.ds(h, tm, stride=N_HEADS), :] = kh
        if transposed:
            ka_ref[:, head_cols(h)] = kh.astype(BF16)

    zv = jnp.dot(xn, wv_ref[...], preferred_element_type=F32)
    for h in range(N_HEADS):
        vh = zv[:, head_cols(h)]
        vo_ref[pl.ds(h, tm, stride=N_HEADS), :] = vh
        if transposed:
            va_ref[head_cols(h), :] = vh.T.astype(BF16)


def _qkv(x, g, w_in, cos_t, sin_t, tm, transposed, layer, depth, earlier):
    n = x.shape[0]
    n_tiles = n // tm
    n_tab = cos_t.shape[0] // tm
    row = lambda i: (i, 0)
    flat = pl.BlockSpec((tm, D_MODEL), row)
    chan = pl.BlockSpec((D_MODEL, tm), lambda i: (0, i))
    if earlier is None:
        assert layer == 0
        cache = pl.BlockSpec((depth, tm * N_HEADS, V_DIM), lambda i: (0, i, 0))
    else:
        cache = pl.BlockSpec((None, tm * N_HEADS, V_DIM), lambda i: (layer, i, 0))
    tab = pl.BlockSpec((tm, LANES), lambda i: (i % n_tab, 0))
    flat_bf16 = jax.ShapeDtypeStruct((n, D_MODEL), BF16)
    chan_bf16 = jax.ShapeDtypeStruct((D_MODEL, n), BF16)
    cache_f32 = jax.ShapeDtypeStruct((depth, n * N_HEADS, V_DIM), F32)
    if transposed:
        out_specs = [chan, flat, chan, cache, cache]
        out_shape = [chan_bf16, flat_bf16, chan_bf16, cache_f32, cache_f32]
    else:
        out_specs = [flat, cache, cache]
        out_shape = [flat_bf16, cache_f32, cache_f32]
    in_specs = [flat, _const_spec((1, D_MODEL)), _in_segment_spec(SEG_Q), _in_segment_spec(SEG_K),
                _in_segment_spec(SEG_V), tab, tab]
    args = [x, g, w_in, w_in, w_in, cos_t, sin_t]
    aliases = {}
    if earlier is not None:
        n_out = len(out_shape)
        aliases = {len(args): n_out - 2, len(args) + 1: n_out - 1}
        in_specs += [pl.BlockSpec(memory_space=pl.ANY)] * 2
        args += list(earlier)
    return pl.pallas_call(
        functools.partial(_qkv_kernel, tm=tm, transposed=transposed, n_aliased=len(aliases)),
        grid=(n_tiles,),
        in_specs=in_specs,
        out_specs=out_specs,
        out_shape=out_shape,
        input_output_aliases=aliases,
        compiler_params=_params(("parallel",)),
        name="qkv",
    )(*args)


def _rnn_kernel(x_ref, g_ref, wxr_ref, wgr_ref, h0_ref, c0_ref, cw_ref, cb_ref,
                wa_ref, ba_ref, wx_ref, bx_ref, lam_ref,
                o_ref, hl_ref, cl_ref,
                xe_ref, gg_ref, al_ref, bl_ref, hin_ref, o32_ref, h_ref, *, T):
    t = pl.program_id(1)
    halo = CONV_W - 1
    G = T // SUBLANES
    NB = D_MODEL // LANES
    blocks = [(c, slice(c * LANES, (c + 1) * LANES)) for c in range(NB)]

    @pl.when(t == 0)
    def _():
        for c, cols in blocks:
            xe_ref[c, SUBLANES - halo:SUBLANES, :] = c0_ref[:, cols]
        h_ref[...] = h0_ref[...]

    @pl.when(t > 0)
    def _():
        for c, _ in blocks:
            xe_ref[c, SUBLANES - halo:SUBLANES, :] = xe_ref[c, T + SUBLANES - halo:T + SUBLANES, :]

    def class_rows(ref, c, start):
        return ref[c, pl.ds(start, G, stride=SUBLANES), :]

    xn = _rmsnorm_bf16(x_ref[...], g_ref[...])
    xr = jnp.dot(xn, wxr_ref[...], preferred_element_type=F32)
    gg = jax.nn.gelu(jnp.dot(xn, wgr_ref[...], preferred_element_type=F32))
    for c, cols in blocks:
        xe_ref[c, SUBLANES:SUBLANES + T, :] = xr[:, cols]
        gg_ref[c] = gg[:, cols]

    def class_block(v, k):
        return v[k * G:(k + 1) * G, :]

    for grp in range(RG_GROUPS):
        gcols = slice(grp * MXU_DIM, (grp + 1) * MXU_DIM)
        gblocks = blocks[grp * (MXU_DIM // LANES):(grp + 1) * (MXU_DIM // LANES)]

        classes = []
        for k in range(SUBLANES):
            parts = []
            for c, cols in gblocks:
                acc = cb_ref[:, cols]
                for j in range(CONV_W):
                    acc = acc + class_rows(xe_ref, c, SUBLANES - halo + j + k) * cw_ref[j:j + 1, cols]
                parts.append(acc)
            classes.append(jnp.concatenate(parts, axis=1))
        xc = jnp.concatenate(classes, axis=0)
        xcb = xc.astype(BF16)

        def gate(w_ref, bias_ref):
            z = jnp.dot(xcb, w_ref[grp], preferred_element_type=F32)
            return jax.nn.sigmoid(z + bias_ref[:, gcols])

        r = gate(wa_ref, ba_ref)
        i = gate(wx_ref, bx_ref)
        log_a = r * ((-RG_C) * jax.nn.softplus(-lam_ref[:, gcols]))
        a = jnp.exp(log_a)
        u = jnp.sqrt(1.0 - a * a) * (i * xc)

        a_cum = class_block(a, 0)
        b_cum = class_block(u, 0)
        al_ref[0, :, gcols] = a_cum
        bl_ref[0, :, gcols] = b_cum
        for k in range(1, SUBLANES):
            ak = class_block(a, k)
            b_cum = ak * b_cum + class_block(u, k)
            a_cum = ak * a_cum
            al_ref[k, :, gcols] = a_cum
            bl_ref[k, :, gcols] = b_cum
        h = h_ref[:, gcols]
        for gi in range(G):
            hin_ref[gi:gi + 1, gcols] = h
            h = bl_ref[SUBLANES - 1, gi:gi + 1, gcols] + al_ref[SUBLANES - 1, gi:gi + 1, gcols] * h
        h_ref[:, gcols] = h
        for c, cols in gblocks:
            h_in = hin_ref[:, cols]
            for k in range(SUBLANES):
                hk = bl_ref[k, :, cols] + al_ref[k, :, cols] * h_in
                o32_ref[c, pl.ds(k, G, stride=SUBLANES), :] = hk * class_rows(gg_ref, c, k)
            o_ref[:, cols] = o32_ref[c].astype(o_ref.dtype)
            cl_ref[:, cols] = xe_ref[c, T + SUBLANES - halo:T + SUBLANES, :]
    hl_ref[...] = h_ref[...]


def _rnn(x, g, w_in, h0, c0, cw, cb, wa4, ba, wx4, bx, lam, T):
    B, S, _ = x.shape
    assert S % T == 0 and T % SUBLANES == 0 and T >= CONV_W - 1
    G = T // SUBLANES
    NB = D_MODEL // LANES
    seq = lambda b, t: (b, t, 0)
    per_b = lambda b, t: (b, 0, 0)
    vec = _const_spec((1, D_MODEL))
    wgate = _const_spec((RG_GROUPS, MXU_DIM, MXU_DIM))
    tile = pltpu.VMEM((NB, T, LANES), F32)
    return pl.pallas_call(
        functools.partial(_rnn_kernel, T=T),
        grid=(B, S // T),
        in_specs=[
            pl.BlockSpec((None, T, D_MODEL), seq),
            vec, _in_segment_spec(SEG_XR), _in_segment_spec(SEG_GR),
            pl.BlockSpec((None, 1, D_MODEL), per_b),
            pl.BlockSpec((None, CONV_W - 1, D_MODEL), per_b),
            _const_spec((CONV_W, D_MODEL)),
            vec, wgate, vec, wgate, vec, vec,
        ],
        out_specs=[
            pl.BlockSpec((None, T, D_MODEL), seq),
            pl.BlockSpec((None, 1, D_MODEL), per_b),
            pl.BlockSpec((None, CONV_W - 1, D_MODEL), per_b),
        ],
        out_shape=[
            jax.ShapeDtypeStruct((B, S, D_MODEL), BF16),
            jax.ShapeDtypeStruct((B, 1, D_MODEL), F32),
            jax.ShapeDtypeStruct((B, CONV_W - 1, D_MODEL), F32),
        ],
        scratch_shapes=[
            pltpu.VMEM((NB, T + SUBLANES, LANES), F32),
            tile,
            pltpu.VMEM((SUBLANES, G, D_MODEL), F32),
            pltpu.VMEM((SUBLANES, G, D_MODEL), F32),
            pltpu.VMEM((G, D_MODEL), F32),
            tile,
            pltpu.VMEM((1, D_MODEL), F32),
        ],
        compiler_params=_params(("parallel", "arbitrary")),
        name="rnn",
    )(x, g, w_in, w_in, h0, c0, cw, cb, wa4, ba, wx4, bx, lam)


def _attn_p_kernel(qt_ref, k_ref, vt_ref, lq1_ref, lk1_ref, lq2_ref, lk2_ref, g_ref,
                   o_ref, q2_ref, s_ref, mx_ref, p_ref, alpha_ref, m_ref, l_ref, acc_ref,
                   *, tq, tk, hps, lambda_init):
    qi = pl.program_id(2)
    n_pairs = qi
    heads = range(hps)

    def head_rows(h):
        return slice(h * V_DIM, (h + 1) * V_DIM)

    row = lax.broadcasted_iota(jnp.int32, (V_DIM, tq), 0)
    for h in heads:
        qt = qt_ref[head_rows(h), :]
        zero = jnp.zeros_like(qt)
        q2_ref[h, :, :tq] = jnp.where(row < HEAD_DIM, qt, zero)
        q2_ref[h, :, tq:] = jnp.where(row >= HEAD_DIM, qt, zero)
    m_ref[...] = jnp.full(m_ref.shape, NEG_INF, F32)
    l_ref[...] = jnp.zeros(l_ref.shape, F32)
    acc_ref[...] = jnp.zeros(acc_ref.shape, F32)
    for h in heads:
        p_ref[h, 1] = jnp.zeros(p_ref.shape[2:], BF16)
        alpha_ref[h, 1] = jnp.ones(alpha_ref.shape[2:], F32)

    def produce(h, kc, slot, masked):
        r0 = pl.multiple_of(kc * tk, tk)
        s = jnp.dot(k_ref[pl.ds(r0, tk), head_rows(h)], q2_ref[h], preferred_element_type=F32)
        if masked:
            key = kc * tk + lax.broadcasted_iota(jnp.int32, s.shape, 0)
            qry = qi * tq + (lax.broadcasted_iota(jnp.int32, s.shape, 1) & (tq - 1))
            s = jnp.where(key <= qry, s, NEG_INF)
        s_ref[h, slot] = s
        mx_ref[h, slot] = jnp.max(s, axis=0, keepdims=True)

    def apply_values(h, kc, slot):
        r0 = pl.multiple_of(kc * tk, tk)
        pv = jnp.dot(vt_ref[head_rows(h), pl.ds(r0, tk)], p_ref[h, slot], preferred_element_type=F32)
        acc_ref[h] = alpha_ref[h, slot] * acc_ref[h] + pv

    def softmax(h, slot):
        m_old = m_ref[h]
        m_new = jnp.maximum(m_old, mx_ref[h, slot])
        alpha = jnp.exp2(m_old - m_new)
        p = jnp.exp2(s_ref[h, slot] - m_new)
        l_ref[h] = alpha * l_ref[h] + jnp.sum(p, axis=0, keepdims=True)
        m_ref[h] = m_new
        alpha_ref[h, slot] = alpha
        p_ref[h, slot] = p.astype(BF16)

    def step(kc, slot, next_masked):
        if next_masked is not None:
            for h in heads:
                produce(h, kc + 1, 1 - slot, next_masked)
        for h in heads:
            apply_values(h, jnp.maximum(kc - 1, 0), 1 - slot)
        for h in heads:
            softmax(h, slot)

    def pair(j, then_masked):
        step(2 * j, 0, False)
        step(2 * j + 1, 1, then_masked)

    for h in heads:
        produce(h, 0, 0, True)

    def body(j, carry):
        pair(j, False)
        return carry

    lax.fori_loop(0, n_pairs - 1, body, 0)

    @pl.when(n_pairs > 0)
    def _():
        pair(n_pairs - 1, True)

    step(2 * n_pairs, 0, True)
    step(2 * n_pairs + 1, 1, None)
    lam = _diff_lambda(lq1_ref, lk1_ref, lq2_ref, lk2_ref, lambda_init)
    for h in heads:
        apply_values(h, 2 * n_pairs + 1, 1)
        on = acc_ref[h] / l_ref[h]
        ot = on[:, :tq] - lam * on[:, tq:]
        o_ref[:, head_rows(h)] = _subln(ot.T, g_ref[...], lambda_init).astype(o_ref.dtype)


def _attn_prompt(qt, k, vt, lq1, lk1, lq2, lk2, g, lambda_init, tq, tk, hps):
    B, S, _ = k.shape
    assert S % tq == 0 and tq == 2 * tk and tq & (tq - 1) == 0 and N_HEADS % hps == 0
    nq = S // tq
    lvec = _const_spec((1, HEAD_DIM))
    slots = lambda shape, dtype: pltpu.VMEM((hps, 2) + shape, dtype)
    return pl.pallas_call(
        functools.partial(_attn_p_kernel, tq=tq, tk=tk, hps=hps, lambda_init=lambda_init),
        grid=(B, N_HEADS // hps, nq),
        in_specs=[
            pl.BlockSpec((hps * V_DIM, tq), lambda b, h, i: (h, b * nq + i)),
            pl.BlockSpec((None, S, hps * V_DIM), lambda b, h, i: (b, 0, h)),
            pl.BlockSpec((hps * V_DIM, S), lambda b, h, i: (h, b)),
            lvec, lvec, lvec, lvec,
            _const_spec((1, V_DIM)),
        ],
        out_specs=pl.BlockSpec((None, tq, hps * V_DIM), lambda b, h, i: (b, i, h)),
        out_shape=jax.ShapeDtypeStruct((B, S, D_MODEL), BF16),
        scratch_shapes=[
            pltpu.VMEM((hps, V_DIM, 2 * tq), BF16),
            slots((tk, 2 * tq), F32),
            slots((1, 2 * tq), F32),
            slots((tk, 2 * tq), BF16),
            slots((1, 2 * tq), F32),
            pltpu.VMEM((hps, 1, 2 * tq), F32),
            pltpu.VMEM((hps, 1, 2 * tq), F32),
            pltpu.VMEM((hps, V_DIM, 2 * tq), F32),
        ],
        compiler_params=_params(("parallel", "parallel", "arbitrary")),
        name="attn_prompt",
    )(qt, k, vt, lq1, lk1, lq2, lk2, g)


def _attn_s_kernel(pt_ref, q_ref, kn_ref, vn_ref, lq1_ref, lk1_ref, lq2_ref, lk2_ref, g_ref, *rest,
                   pp, T, lambda_init):
    k_refs = rest[:pp]
    v_refs = rest[pp:2 * pp]
    o_ref, qm_ref, m_ref, l_ref, acc_ref = rest[2 * pp:]
    del pt_ref
    p = pl.program_id(1)
    n_rows = N_HEADS * 2 * T

    def token_major(ref, n_tokens):
        return jnp.concatenate([ref[pl.ds(h, n_tokens, stride=N_HEADS), :] for h in range(N_HEADS)], axis=1)

    @pl.when(p == 0)
    def _():
        q = q_ref[...].astype(F32)
        qrep = jnp.concatenate([q] * (2 * N_HEADS), axis=0)
        rr = lax.broadcasted_iota(jnp.int32, qrep.shape, 0)
        cc = lax.broadcasted_iota(jnp.int32, qrep.shape, 1)
        qm_ref[...] = jnp.where(cc // HEAD_DIM == rr // T, qrep, 0.0).astype(BF16)
        m_ref[...] = jnp.full(m_ref.shape, NEG_INF, F32)
        l_ref[...] = jnp.zeros(l_ref.shape, F32)
        acc_ref[...] = jnp.zeros(acc_ref.shape, F32)

    qm = qm_ref[...]

    def scores(kb):
        return lax.dot_general(qm, kb, _NT, preferred_element_type=F32)

    def fold(s, v_list):
        width = s.shape[1] // len(v_list)
        m_old = m_ref[...]
        m_new = jnp.maximum(m_old, jnp.max(s, axis=-1, keepdims=True))
        alpha = jnp.exp2(m_old - m_new)
        pb = jnp.exp2(s - m_new)
        l_ref[...] = alpha * l_ref[...] + jnp.sum(pb, axis=-1, keepdims=True)
        pb = pb.astype(BF16)
        pv = None
        for i, vb in enumerate(v_list):
            part = jnp.dot(pb[:, i * width:(i + 1) * width], vb, preferred_element_type=F32)
            pv = part if pv is None else pv + part
        acc_ref[...] = alpha * acc_ref[...] + pv
        m_ref[...] = m_new

    s_pages = [scores(token_major(k_refs[i], PAGE_SIZE).astype(BF16)) for i in range(pp)]
    for a in range(0, pp, pp // 2):
        group = range(a, a + pp // 2)
        fold(jnp.concatenate([s_pages[i] for i in group], axis=1),
             [token_major(v_refs[i], PAGE_SIZE).astype(BF16) for i in group])

    @pl.when(p == pl.num_programs(1) - 1)
    def _():
        pad = jnp.zeros((LANES - T, D_MODEL), F32)
        kn = jnp.concatenate([token_major(kn_ref, T), pad], axis=0).astype(BF16)
        vn = jnp.concatenate([token_major(vn_ref, T), pad], axis=0).astype(BF16)
        s = scores(kn)
        qry = lax.broadcasted_iota(jnp.int32, s.shape, 0) % T
        key = lax.broadcasted_iota(jnp.int32, s.shape, 1)
        fold(jnp.where(key <= qry, s, NEG_INF), [vn])

        lam = _diff_lambda(lq1_ref, lk1_ref, lq2_ref, lk2_ref, lambda_init)
        for h in range(N_HEADS):
            r0 = h * 2 * T
            own = slice(h * V_DIM, (h + 1) * V_DIM)
            on = acc_ref[r0:r0 + 2 * T, own] / l_ref[r0:r0 + 2 * T, :]
            dh = on[:T] - lam * on[T:]
            o_ref[:, own] = _subln(dh, g_ref[...], lambda_init).astype(o_ref.dtype)


def _attn_sample(q, k_new, v_new, cache_k, cache_v, layer, page_table, lq1, lk1, lq2, lk2, g,
                 lambda_init, pp):
    DB, T, _ = q.shape
    n_pages = page_table.shape[1]
    assert n_pages % pp == 0 and pp % 2 == 0 and T == SUBLANES and T * N_HEADS <= LANES
    n_rows = N_HEADS * 2 * T
    pt_flat = page_table.reshape(-1)
    seq = pl.BlockSpec((None, T, D_MODEL), lambda b, p, pt: (b, 0, 0))
    new = pl.BlockSpec((None, T * N_HEADS, V_DIM), lambda b, p, pt: (b, 0, 0))
    lvec = pl.BlockSpec((1, HEAD_DIM), lambda b, p, pt: (0, 0))

    def page_spec(i):
        return pl.BlockSpec((None, None, PAGE_SIZE * N_HEADS, V_DIM),
                            lambda b, p, pt: (layer, pt[b * n_pages + p * pp + i], 0, 0))

    grid_spec = pltpu.PrefetchScalarGridSpec(
        num_scalar_prefetch=1,
        grid=(DB, n_pages // pp),
        in_specs=[seq, new, new, lvec, lvec, lvec, lvec,
                  pl.BlockSpec((1, V_DIM), lambda b, p, pt: (0, 0))]
                 + [page_spec(i) for i in range(pp)] * 2,
        out_specs=seq,
        scratch_shapes=[
            pltpu.VMEM((n_rows, D_MODEL), BF16),
            pltpu.VMEM((n_rows, 1), F32),
            pltpu.VMEM((n_rows, 1), F32),
            pltpu.VMEM((n_rows, D_MODEL), F32),
        ],
    )
    return pl.pallas_call(
        functools.partial(_attn_s_kernel, pp=pp, T=T, lambda_init=lambda_init),
        grid_spec=grid_spec,
        out_shape=jax.ShapeDtypeStruct((DB, T, D_MODEL), BF16),
        compiler_params=_params(("parallel", "arbitrary")),
        name="attn_sample",
    )(pt_flat, q, k_new, v_new, lq1, lk1, lq2, lk2, g, *([cache_k] * pp), *([cache_v] * pp))


def _tail_kernel(x_ref, orn_ref, oat_ref, g1_ref, wgr_ref, wga_ref, wbr_ref, wba_ref, wo_ref,
                 g2_ref, wu_ref, wd_ref, gf_ref, o_ref, *, final):
    x = x_ref[...]
    xn = _rmsnorm_bf16(x, g1_ref[...])
    y = jax.nn.sigmoid(jnp.dot(xn, wgr_ref[...], preferred_element_type=F32)) * \
        jnp.dot(orn_ref[...], wbr_ref[...], preferred_element_type=F32)
    y = y + jax.nn.sigmoid(jnp.dot(xn, wga_ref[...], preferred_element_type=F32)) * \
        jnp.dot(oat_ref[...], wba_ref[...], preferred_element_type=F32)
    acc = x + jnp.dot(y.astype(BF16), wo_ref[...], preferred_element_type=F32)
    xn2 = _rmsnorm_bf16(acc, g2_ref[...])
    for c in range(D_FF // D_MODEL):
        cols = slice(c * D_MODEL, (c + 1) * D_MODEL)
        hid = jnp.maximum(jnp.dot(xn2, wu_ref[:, cols], preferred_element_type=F32), 0.0)
        acc = acc + jnp.dot((hid * hid).astype(BF16), wd_ref[cols, :], preferred_element_type=F32)
    if final:
        ms = jnp.mean(acc * acc, axis=-1, keepdims=True)
        acc = acc * lax.rsqrt(ms + NORM_EPS) * gf_ref[...]
    o_ref[...] = acc


def _tail(x, o_rnn, o_attn, g1, w_in, w_branch, wo, g2, wu, wd, gf, final, tm):
    n = x.shape[0]
    row = pl.BlockSpec((tm, D_MODEL), lambda i: (i, 0))
    vec = _const_spec((1, D_MODEL))
    w = _const_spec((D_MODEL, D_MODEL))
    return pl.pallas_call(
        functools.partial(_tail_kernel, final=final),
        grid=(n // tm,),
        in_specs=[row, row, row, vec, _in_segment_spec(SEG_GRNN), _in_segment_spec(SEG_GATTN),
                  _const_spec((D_MODEL, D_MODEL), (0, 0)), _const_spec((D_MODEL, D_MODEL), (1, 0)), w, vec,
                  _const_spec((D_MODEL, D_FF)), _const_spec((D_FF, D_MODEL)), vec],
        out_specs=row,
        out_shape=jax.ShapeDtypeStruct((n, D_MODEL), F32),
        compiler_params=_params(("parallel",)),
        name="tail",
    )(x, o_rnn, o_attn, g1, w_in, w_in, w_branch, w_branch, wo, g2, wu, wd, gf)


def _rope_tables(pos):
    half = HEAD_DIM // 2
    inv = ROPE_THETA ** (-2.0 * jnp.arange(half, dtype=F32) / HEAD_DIM)
    ang = pos.astype(F32)[:, None] * inv[None, :]
    cos = jnp.cos(ang)
    sin = jnp.sin(ang)
    reps = LANES // HEAD_DIM
    return (jnp.concatenate([cos, cos] * reps, axis=1),
            jnp.concatenate([-sin, sin] * reps, axis=1))


def _gate_tiles(w):
    per = MXU_DIM // RG_BLOCK
    w5 = w.reshape(RG_GROUPS, per, RG_BLOCK, RG_BLOCK)
    eye = jnp.eye(per, dtype=w.dtype)
    t = w5[:, :, :, None, :] * eye[None, :, None, :, None]
    return t.reshape(RG_GROUPS, MXU_DIM, MXU_DIM).astype(BF16)


def kernel(x_prompt, x_sample, cache_k, cache_v, state_h, state_conv, page_table, ln1_g, w_in, conv_w, conv_b, rg_wa, rg_ba, rg_wx, rg_bx, rg_lambda, lam_q1, lam_k1, lam_q2, lam_k2, subln_g, w_branch, w_out, ln2_g, w_up, w_down, final_g):
    B, S, _ = x_prompt.shape
    DB, T, _ = x_sample.shape
    depth = w_in.shape[0]
    n_pool = cache_k.shape[1]
    past_len = page_table.shape[1] * PAGE_SIZE

    cos_p, sin_p = _rope_tables(jnp.arange(S))
    cos_s, sin_s = _rope_tables(past_len + jnp.arange(DB * T) % T)
    ck = cache_k.reshape(depth, n_pool, PAGE_SIZE * N_HEADS, V_DIM)
    cv = cache_v.reshape(depth, n_pool, PAGE_SIZE * N_HEADS, V_DIM)

    xp = x_prompt.reshape(B * S, D_MODEL)
    xs = x_sample.reshape(DB * T, D_MODEL)
    h0_p = jnp.zeros((B, 1, D_MODEL), F32)
    c0_p = jnp.zeros((B, CONV_W - 1, D_MODEL), F32)
    gf = final_g.reshape(1, D_MODEL)

    outs = [[] for _ in range(4)]
    kv_p = kv_s = None
    for l in range(depth):
        lambda_init = 0.8 - 0.6 * math.exp(-0.3 * l)
        w_in_bf = w_in[l].astype(BF16)
        w_branch_bf = w_branch[l].astype(BF16)
        wo = w_out[l].astype(BF16)
        wu = w_up[l].astype(BF16)
        wd = w_down[l].astype(BF16)
        wa4 = _gate_tiles(rg_wa[l])
        wx4 = _gate_tiles(rg_wx[l])
        vec = lambda a: a[l].reshape(1, -1)
        lam_vecs = (vec(lam_q1), vec(lam_k1), vec(lam_q2), vec(lam_k2))
        final = l == depth - 1

        def layer(x, n_seq, seq_len, cos_t, sin_t, h0, c0, tm, t_rnn, prompt, kv_all):
            qkv = _qkv(x, vec(ln1_g), w_in_bf, cos_t, sin_t, tm, prompt, l, depth, kv_all)
            kv_all = qkv[-2:]
            o_rnn, h_last, c_last = _rnn(x.reshape(n_seq, seq_len, D_MODEL), vec(ln1_g), w_in_bf, h0, c0,
                                         conv_w[l], vec(conv_b), wa4, vec(rg_ba), wx4, vec(rg_bx),
                                         vec(rg_lambda), t_rnn)
            if prompt:
                qt, ka, vt = qkv[:3]
                o_attn = _attn_prompt(qt, ka.reshape(n_seq, seq_len, D_MODEL), vt, *lam_vecs, vec(subln_g),
                                      lambda_init, ATTN_TQ, ATTN_TK, ATTN_HEADS_PER_STEP)
            else:
                new_rows = lambda a: a[l].reshape(n_seq, seq_len * N_HEADS, V_DIM)
                o_attn = _attn_sample(qkv[0].reshape(n_seq, seq_len, D_MODEL), new_rows(kv_all[0]),
                                      new_rows(kv_all[1]), ck, cv, l, page_table, *lam_vecs, vec(subln_g),
                                      lambda_init, PAGES_PER_STEP)
            x2 = _tail(x, o_rnn.reshape(-1, D_MODEL), o_attn.reshape(-1, D_MODEL), vec(ln1_g),
                       w_in_bf, w_branch_bf, wo, vec(ln2_g), wu, wd, gf, final, tm)
            return x2, kv_all, h_last.reshape(n_seq, D_MODEL), c_last

        xp, kv_p, hp, cp = layer(xp, B, S, cos_p, sin_p, h0_p, c0_p, ROW_TILE, RNN_TILE, True, kv_p)
        xs, kv_s, hs, cs = layer(xs, DB, T, cos_s, sin_s, state_h[l].reshape(DB, 1, D_MODEL),
                                 state_conv[l], DB * T, T, False, kv_s)
        for lst, val in zip(outs, (hp, cp, hs, cs)):
            lst.append(val)

    h_p, c_p, h_s, c_s = (jnp.stack(o) for o in outs)
    as_cache = lambda a, n_seq, seq_len: a.reshape(depth, n_seq, seq_len, N_HEADS, V_DIM)
    return (xp.reshape(B, S, D_MODEL), xs.reshape(DB, T, D_MODEL),
            as_cache(kv_p[0], B, S), as_cache(kv_p[1], B, S), h_p, c_p,
            as_cache(kv_s[0], DB, T), as_cache(kv_s[1], DB, T), h_s, c_s)
```

```python
import functools
import math

import jax
import jax.numpy as jnp
from jax import lax
from jax.experimental import pallas as pl
from jax.experimental.pallas import tpu as pltpu

F32 = jnp.float32
BF16 = jnp.bfloat16

D_MODEL = 1024
N_HEADS = 8
HEAD_DIM = 64
V_DIM = 2 * HEAD_DIM
PAGE_SIZE = 128
ROPE_THETA = 10000.0
SUBLN_EPS = 1e-5
N_RG_BLOCKS = 16
RG_BLOCK = D_MODEL // N_RG_BLOCKS
CONV_W = 4
RG_C = 8.0
D_FF = 4 * D_MODEL
NORM_EPS = 1e-6
ATTN_SCALE = HEAD_DIM ** -0.5
LOG2_E = math.log2(math.e)
SEG_XR, SEG_GR, SEG_Q, SEG_K, SEG_V, SEG_GRNN, SEG_GATTN = range(7)

LANES = 128
SUBLANES = 8
MXU_DIM = 256
RG_GROUPS = D_MODEL // MXU_DIM
VMEM_LIMIT = 58 * 1024 * 1024

ROW_TILE = 512
RNN_TILE = 256
ATTN_TQ, ATTN_TK = 512, 256
ATTN_HEADS_PER_STEP = 4
PAGES_PER_STEP = 16

NEG_INF = float("-inf")
_NT = (((1,), (1,)), ((), ()))


def _params(semantics):
    return pltpu.CompilerParams(dimension_semantics=semantics, vmem_limit_bytes=VMEM_LIMIT)


def _const_spec(shape, index=None):
    index = (0,) * len(shape) if index is None else index
    return pl.BlockSpec(shape, lambda *_: index, pipeline_mode=pl.Buffered(1))


def _in_segment_spec(segment):
    return _const_spec((D_MODEL, D_MODEL), (0, segment))


def _rmsnorm_bf16(x, g):
    ms = jnp.mean(x * x, axis=-1, keepdims=True)
    return (x * lax.rsqrt(ms + NORM_EPS) * g).astype(BF16)


def _diff_lambda(lq1_ref, lk1_ref, lq2_ref, lk2_ref, lambda_init):
    s1 = jnp.sum(lq1_ref[...] * lk1_ref[...], axis=-1, keepdims=True)
    s2 = jnp.sum(lq2_ref[...] * lk2_ref[...], axis=-1, keepdims=True)
    return jnp.exp(s1) - jnp.exp(s2) + lambda_init


def _subln(o, g, lambda_init):
    ms = jnp.mean(o * o, axis=-1, keepdims=True)
    return (o * lax.rsqrt(ms + SUBLN_EPS) * g) * (1.0 - lambda_init)


def _qkv_kernel(x_ref, g_ref, wq_ref, wk_ref, wv_ref, cos_ref, sin_ref, *refs, tm, transposed, n_aliased):
    out_refs = refs[n_aliased:]
    if transposed:
        q_ref, ka_ref, va_ref, ko_ref, vo_ref = out_refs
    else:
        q_ref, ko_ref, vo_ref = out_refs
    if n_aliased == 0:
        for ref in (ko_ref, vo_ref):
            if ref.shape[0] > 1:
                ref[1:] = jnp.zeros((ref.shape[0] - 1,) + ref.shape[1:], F32)
        ko_ref, vo_ref = ko_ref.at[0], vo_ref.at[0]
    xn = _rmsnorm_bf16(x_ref[...], g_ref[...])
    cos = cos_ref[...]
    sin = sin_ref[...]
    lane = lax.broadcasted_iota(jnp.int32, cos.shape, 1)
    lower = (lane & (HEAD_DIM // 2)) == 0

    def head_cols(h):
        return slice(h * V_DIM, (h + 1) * V_DIM)

    def rope_heads(z):
        for h in range(N_HEADS):
            zh = z[:, head_cols(h)]
            partner = jnp.where(lower,
                                pltpu.roll(zh, LANES - HEAD_DIM // 2, 1),
                                pltpu.roll(zh, HEAD_DIM // 2, 1))
            yield h, zh * cos + partner * sin

    zq = jnp.dot(xn, wq_ref[...], preferred_element_type=F32)
    for h, qh in rope_heads(zq):
        qh = qh * (ATTN_SCALE * LOG2_E)
        if transposed:
            q_ref[head_cols(h), :] = qh.T.astype(BF16)
        else:
            q_ref[:, head_cols(h)] = qh.astype(BF16)

    zk = jnp.dot(xn, wk_ref[...], preferred_element_type=F32)
    for h, kh in rope_heads(zk):
        ko_ref[pl.ds(h, tm, stride=N_HEADS), :] = kh
        if transposed:
            ka_ref[:, head_cols(h)] = kh.astype(BF16)

    zv = jnp.dot(xn, wv_ref[...], preferred_element_type=F32)
    for h in range(N_HEADS):
        vh = zv[:, head_cols(h)]
        vo_ref[pl.ds(h, tm, stride=N_HEADS), :] = vh
        if transposed:
            va_ref[head_cols(h), :] = vh.T.astype(BF16)


def _qkv(x, g, w_in, cos_t, sin_t, tm, transposed, layer, depth, earlier):
    n = x.shape[0]
    n_tiles = n // tm
    n_tab = cos_t.shape[0] // tm
    row = lambda i: (i, 0)
    flat = pl.BlockSpec((tm, D_MODEL), row)
    chan = pl.BlockSpec((D_MODEL, tm), lambda i: (0, i))
    if earlier is None:
        assert layer == 0
        cache = pl.BlockSpec((depth, tm * N_HEADS, V_DIM), lambda i: (0, i, 0))
    else:
        cache = pl.BlockSpec((None, tm * N_HEADS, V_DIM), lambda i: (layer, i, 0))
    tab = pl.BlockSpec((tm, LANES), lambda i: (i % n_tab, 0))
    flat_bf16 = jax.ShapeDtypeStruct((n, D_MODEL), BF16)
    chan_bf16 = jax.ShapeDtypeStruct((D_MODEL, n), BF16)
    cache_f32 = jax.ShapeDtypeStruct((depth, n * N_HEADS, V_DIM), F32)
    if transposed:
        out_specs = [chan, flat, chan, cache, cache]
        out_shape = [chan_bf16, flat_bf16, chan_bf16, cache_f32, cache_f32]
    else:
        out_specs = [flat, cache, cache]
        out_shape = [flat_bf16, cache_f32, cache_f32]
    in_specs = [flat, _const_spec((1, D_MODEL)), _in_segment_spec(SEG_Q), _in_segment_spec(SEG_K),
                _in_segment_spec(SEG_V), tab, tab]
    args = [x, g, w_in, w_in, w_in, cos_t, sin_t]
    aliases = {}
    if earlier is not None:
        n_out = len(out_shape)
        aliases = {len(args): n_out - 2, len(args) + 1: n_out - 1}
        in_specs += [pl.BlockSpec(memory_space=pl.ANY)] * 2
        args += list(earlier)
    return pl.pallas_call(
        functools.partial(_qkv_kernel, tm=tm, transposed=transposed, n_aliased=len(aliases)),
        grid=(n_tiles,),
        in_specs=in_specs,
        out_specs=out_specs,
        out_shape=out_shape,
        input_output_aliases=aliases,
        compiler_params=_params(("parallel",)),
        name="qkv",
    )(*args)


def _rnn_kernel(x_ref, g_ref, wxr_ref, wgr_ref, h0_ref, c0_ref, cw_ref, cb_ref,
                wa_ref, ba_ref, wx_ref, bx_ref, lam_ref,
                o_ref, hl_ref, cl_ref,
                xe_ref, gg_ref, al_ref, bl_ref, hin_ref, o32_ref, h_ref, *, T):
    t = pl.program_id(1)
    halo = CONV_W - 1
    G = T // SUBLANES
    NB = D_MODEL // LANES
    blocks = [(c, slice(c * LANES, (c + 1) * LANES)) for c in range(NB)]

    @pl.when(t == 0)
    def _():
        for c, cols in blocks:
            xe_ref[c, SUBLANES - halo:SUBLANES, :] = c0_ref[:, cols]
        h_ref[...] = h0_ref[...]

    @pl.when(t > 0)
    def _():
        for c, _ in blocks:
            xe_ref[c, SUBLANES - halo:SUBLANES, :] = xe_ref[c, T + SUBLANES - halo:T + SUBLANES, :]

    def class_rows(ref, c, start):
        return ref[c, pl.ds(start, G, stride=SUBLANES), :]

    xn = _rmsnorm_bf16(x_ref[...], g_ref[...])
    xr = jnp.dot(xn, wxr_ref[...], preferred_element_type=F32)
    gg = jax.nn.gelu(jnp.dot(xn, wgr_ref[...], preferred_element_type=F32))
    for c, cols in blocks:
        xe_ref[c, SUBLANES:SUBLANES + T, :] = xr[:, cols]
        gg_ref[c] = gg[:, cols]

    def class_block(v, k):
        return v[k * G:(k + 1) * G, :]

    for grp in range(RG_GROUPS):
        gcols = slice(grp * MXU_DIM, (grp + 1) * MXU_DIM)
        gblocks = blocks[grp * (MXU_DIM // LANES):(grp + 1) * (MXU_DIM // LANES)]

        classes = []
        for k in range(SUBLANES):
            parts = []
            for c, cols in gblocks:
                acc = cb_ref[:, cols]
                for j in range(CONV_W):
                    acc = acc + class_rows(xe_ref, c, SUBLANES - halo + j + k) * cw_ref[j:j + 1, cols]
                parts.append(acc)
            classes.append(jnp.concatenate(parts, axis=1))
        xc = jnp.concatenate(classes, axis=0)
        xcb = xc.astype(BF16)

        def gate(w_ref, bias_ref):
            z = jnp.dot(xcb, w_ref[grp], preferred_element_type=F32)
            return jax.nn.sigmoid(z + bias_ref[:, gcols])

        r = gate(wa_ref, ba_ref)
        i = gate(wx_ref, bx_ref)
        log_a = r * ((-RG_C) * jax.nn.softplus(-lam_ref[:, gcols]))
        a = jnp.exp(log_a)
        u = jnp.sqrt(1.0 - a * a) * (i * xc)

        a_cum = class_block(a, 0)
        b_cum = class_block(u, 0)
        al_ref[0, :, gcols] = a_cum
        bl_ref[0, :, gcols] = b_cum
        for k in range(1, SUBLANES):
            ak = class_block(a, k)
            b_cum = ak * b_cum + class_block(u, k)
            a_cum = ak * a_cum
            al_ref[k, :, gcols] = a_cum
            bl_ref[k, :, gcols] = b_cum
        h = h_ref[:, gcols]
        for gi in range(G):
            hin_ref[gi:gi + 1, gcols] = h
            h = bl_ref[SUBLANES - 1, gi:gi + 1, gcols] + al_ref[SUBLANES - 1, gi:gi + 1, gcols] * h
        h_ref[:, gcols] = h
        for c, cols in gblocks:
            h_in = hin_ref[:, cols]
            for k in range(SUBLANES):
                hk = bl_ref[k, :, cols] + al_ref[k, :, cols] * h_in
                o32_ref[c, pl.ds(k, G, stride=SUBLANES), :] = hk * class_rows(gg_ref, c, k)
            o_ref[:, cols] = o32_ref[c].astype(o_ref.dtype)
            cl_ref[:, cols] = xe_ref[c, T + SUBLANES - halo:T + SUBLANES, :]
    hl_ref[...] = h_ref[...]


def _rnn(x, g, w_in, h0, c0, cw, cb, wa4, ba, wx4, bx, lam, T):
    B, S, _ = x.shape
    assert S % T == 0 and T % SUBLANES == 0 and T >= CONV_W - 1
    G = T // SUBLANES
    NB = D_MODEL // LANES
    seq = lambda b, t: (b, t, 0)
    per_b = lambda b, t: (b, 0, 0)
    vec = _const_spec((1, D_MODEL))
    wgate = _const_spec((RG_GROUPS, MXU_DIM, MXU_DIM))
    tile = pltpu.VMEM((NB, T, LANES), F32)
    return pl.pallas_call(
        functools.partial(_rnn_kernel, T=T),
        grid=(B, S // T),
        in_specs=[
            pl.BlockSpec((None, T, D_MODEL), seq),
            vec, _in_segment_spec(SEG_XR), _in_segment_spec(SEG_GR),
            pl.BlockSpec((None, 1, D_MODEL), per_b),
            pl.BlockSpec((None, CONV_W - 1, D_MODEL), per_b),
            _const_spec((CONV_W, D_MODEL)),
            vec, wgate, vec, wgate, vec, vec,
        ],
        out_specs=[
            pl.BlockSpec((None, T, D_MODEL), seq),
            pl.BlockSpec((None, 1, D_MODEL), per_b),
            pl.BlockSpec((None, CONV_W - 1, D_MODEL), per_b),
        ],
        out_shape=[
            jax.ShapeDtypeStruct((B, S, D_MODEL), BF16),
            jax.ShapeDtypeStruct((B, 1, D_MODEL), F32),
            jax.ShapeDtypeStruct((B, CONV_W - 1, D_MODEL), F32),
        ],
        scratch_shapes=[
            pltpu.VMEM((NB, T + SUBLANES, LANES), F32),
            tile,
            pltpu.VMEM((SUBLANES, G, D_MODEL), F32),
            pltpu.VMEM((SUBLANES, G, D_MODEL), F32),
            pltpu.VMEM((G, D_MODEL), F32),
            tile,
            pltpu.VMEM((1, D_MODEL), F32),
        ],
        compiler_params=_params(("parallel", "arbitrary")),
        name="rnn",
    )(x, g, w_in, w_in, h0, c0, cw, cb, wa4, ba, wx4, bx, lam)


def _attn_p_kernel(qt_ref, k_ref, vt_ref, bias_ref, lq1_ref, lk1_ref, lq2_ref, lk2_ref, g_ref,
                   o_ref, q2_ref, s_ref, mx_ref, p_ref, alpha_ref, m_ref, acc_ref,
                   *, tq, tk, hps, lambda_init):
    qi = pl.program_id(2)
    n_pairs = qi
    heads = range(hps)
    ones = jnp.ones((acc_ref.shape[1] - V_DIM, tk), BF16)

    def head_rows(h):
        return slice(h * V_DIM, (h + 1) * V_DIM)

    row = lax.broadcasted_iota(jnp.int32, (V_DIM, tq), 0)
    for h in heads:
        qt = qt_ref[head_rows(h), :]
        zero = jnp.zeros_like(qt)
        q2_ref[h, :, :tq] = jnp.where(row < HEAD_DIM, qt, zero)
        q2_ref[h, :, tq:] = jnp.where(row >= HEAD_DIM, qt, zero)
    m_ref[...] = jnp.full(m_ref.shape, NEG_INF, F32)
    acc_ref[...] = jnp.zeros(acc_ref.shape, F32)
    for h in heads:
        p_ref[h, 1] = jnp.zeros(p_ref.shape[2:], BF16)
        alpha_ref[h, 1] = jnp.ones(alpha_ref.shape[2:], F32)

    def produce(h, kc, slot, diag):
        r0 = pl.multiple_of(kc * tk, tk)
        s = jnp.dot(k_ref[pl.ds(r0, tk), head_rows(h)], q2_ref[h], preferred_element_type=F32)
        if diag is not None:
            s = s + bias_ref[diag]
        s_ref[h, slot] = s
        mx_ref[h, slot] = jnp.max(s, axis=0, keepdims=True)

    def apply_values(h, kc, slot):
        r0 = pl.multiple_of(kc * tk, tk)
        vt1 = jnp.concatenate([vt_ref[head_rows(h), pl.ds(r0, tk)], ones], axis=0)
        pv = jnp.dot(vt1, p_ref[h, slot], preferred_element_type=F32)
        acc_ref[h] = alpha_ref[h, slot] * acc_ref[h] + pv

    def softmax(h, slot):
        m_old = m_ref[h]
        m_new = jnp.maximum(m_old, mx_ref[h, slot])
        m_ref[h] = m_new
        alpha_ref[h, slot] = jnp.exp2(m_old - m_new)
        p_ref[h, slot] = jnp.exp2(s_ref[h, slot] - m_new).astype(BF16)

    def step(kc, slot, next_diag):
        if next_diag is not False:
            for h in heads:
                produce(h, kc + 1, 1 - slot, next_diag)
        for h in heads:
            apply_values(h, jnp.maximum(kc - 1, 0), 1 - slot)
        for h in heads:
            softmax(h, slot)

    def pair(j, then_diag):
        step(2 * j, 0, None)
        step(2 * j + 1, 1, then_diag)

    @pl.when(n_pairs == 0)
    def _():
        for h in heads:
            produce(h, 0, 0, 0)

    @pl.when(n_pairs > 0)
    def _():
        for h in heads:
            produce(h, 0, 0, None)

    def body(j, carry):
        pair(j, None)
        return carry

    lax.fori_loop(0, n_pairs - 1, body, 0)

    @pl.when(n_pairs > 0)
    def _():
        pair(n_pairs - 1, 0)

    step(2 * n_pairs, 0, 1)
    step(2 * n_pairs + 1, 1, False)
    lam = _diff_lambda(lq1_ref, lk1_ref, lq2_ref, lk2_ref, lambda_init)
    for h in heads:
        apply_values(h, 2 * n_pairs + 1, 1)
        on = acc_ref[h, :V_DIM] / acc_ref[h, V_DIM:V_DIM + 1]
        ot = on[:, :tq] - lam * on[:, tq:]
        o_ref[:, head_rows(h)] = _subln(ot.T, g_ref[...], lambda_init).astype(o_ref.dtype)


def _attn_prompt(qt, k, vt, lq1, lk1, lq2, lk2, g, lambda_init, tq, tk, hps):
    B, S, _ = k.shape
    assert S % tq == 0 and tq == 2 * tk and tq & (tq - 1) == 0 and N_HEADS % hps == 0
    nq = S // tq
    lvec = _const_spec((1, HEAD_DIM))
    slots = lambda shape, dtype: pltpu.VMEM((hps, 2) + shape, dtype)
    key = lax.broadcasted_iota(jnp.int32, (2, tk, 2 * tq), 0) * tk + lax.broadcasted_iota(jnp.int32, (2, tk, 2 * tq), 1)
    qry = lax.broadcasted_iota(jnp.int32, (2, tk, 2 * tq), 2) & (tq - 1)
    bias = jnp.where(key <= qry, 0.0, NEG_INF).astype(F32)
    ones_rows = 2 * SUBLANES
    return pl.pallas_call(
        functools.partial(_attn_p_kernel, tq=tq, tk=tk, hps=hps, lambda_init=lambda_init),
        grid=(B, N_HEADS // hps, nq),
        in_specs=[
            pl.BlockSpec((hps * V_DIM, tq), lambda b, h, i: (h, b * nq + i)),
            pl.BlockSpec((None, S, hps * V_DIM), lambda b, h, i: (b, 0, h)),
            pl.BlockSpec((hps * V_DIM, S), lambda b, h, i: (h, b)),
            _const_spec((2, tk, 2 * tq)),
            lvec, lvec, lvec, lvec,
            _const_spec((1, V_DIM)),
        ],
        out_specs=pl.BlockSpec((None, tq, hps * V_DIM), lambda b, h, i: (b, i, h)),
        out_shape=jax.ShapeDtypeStruct((B, S, D_MODEL), BF16),
        scratch_shapes=[
            pltpu.VMEM((hps, V_DIM, 2 * tq), BF16),
            slots((tk, 2 * tq), F32),
            slots((1, 2 * tq), F32),
            slots((tk, 2 * tq), BF16),
            slots((1, 2 * tq), F32),
            pltpu.VMEM((hps, 1, 2 * tq), F32),
            pltpu.VMEM((hps, V_DIM + ones_rows, 2 * tq), F32),
        ],
        compiler_params=_params(("parallel", "parallel", "arbitrary")),
        name="attn_prompt",
    )(qt, k, vt, bias, lq1, lk1, lq2, lk2, g)


def _attn_s_kernel(pt_ref, q_ref, kn_ref, vn_ref, lq1_ref, lk1_ref, lq2_ref, lk2_ref, g_ref, *rest,
                   pp, T, lambda_init):
    k_refs = rest[:pp]
    v_refs = rest[pp:2 * pp]
    o_ref, qm_ref, m_ref, l_ref, acc_ref = rest[2 * pp:]
    del pt_ref
    p = pl.program_id(1)
    n_rows = N_HEADS * 2 * T

    def token_major(ref, n_tokens):
        return jnp.concatenate([ref[pl.ds(h, n_tokens, stride=N_HEADS), :] for h in range(N_HEADS)], axis=1)

    @pl.when(p == 0)
    def _():
        q = q_ref[...].astype(F32)
        qrep = jnp.concatenate([q] * (2 * N_HEADS), axis=0)
        rr = lax.broadcasted_iota(jnp.int32, qrep.shape, 0)
        cc = lax.broadcasted_iota(jnp.int32, qrep.shape, 1)
        qm_ref[...] = jnp.where(cc // HEAD_DIM == rr // T, qrep, 0.0).astype(BF16)
        m_ref[...] = jnp.full(m_ref.shape, NEG_INF, F32)
        l_ref[...] = jnp.zeros(l_ref.shape, F32)
        acc_ref[...] = jnp.zeros(acc_ref.shape, F32)

    qm = qm_ref[...]

    def scores(kb):
        return lax.dot_general(qm, kb, _NT, preferred_element_type=F32)

    def fold(s, v_list):
        width = s.shape[1] // len(v_list)
        m_old = m_ref[...]
        m_new = jnp.maximum(m_old, jnp.max(s, axis=-1, keepdims=True))
        alpha = jnp.exp2(m_old - m_new)
        pb = jnp.exp2(s - m_new)
        l_ref[...] = alpha * l_ref[...] + jnp.sum(pb, axis=-1, keepdims=True)
        pb = pb.astype(BF16)
        pv = None
        for i, vb in enumerate(v_list):
            part = jnp.dot(pb[:, i * width:(i + 1) * width], vb, preferred_element_type=F32)
            pv = part if pv is None else pv + part
        acc_ref[...] = alpha * acc_ref[...] + pv
        m_ref[...] = m_new

    s_pages = [scores(token_major(k_refs[i], PAGE_SIZE).astype(BF16)) for i in range(pp)]
    for a in range(0, pp, pp // 2):
        group = range(a, a + pp // 2)
        fold(jnp.concatenate([s_pages[i] for i in group], axis=1),
             [token_major(v_refs[i], PAGE_SIZE).astype(BF16) for i in group])

    @pl.when(p == pl.num_programs(1) - 1)
    def _():
        pad = jnp.zeros((LANES - T, D_MODEL), F32)
        kn = jnp.concatenate([token_major(kn_ref, T), pad], axis=0).astype(BF16)
        vn = jnp.concatenate([token_major(vn_ref, T), pad], axis=0).astype(BF16)
        s = scores(kn)
        qry = lax.broadcasted_iota(jnp.int32, s.shape, 0) % T
        key = lax.broadcasted_iota(jnp.int32, s.shape, 1)
        fold(jnp.where(key <= qry, s, NEG_INF), [vn])

        lam = _diff_lambda(lq1_ref, lk1_ref, lq2_ref, lk2_ref, lambda_init)
        for h in range(N_HEADS):
            r0 = h * 2 * T
            own = slice(h * V_DIM, (h + 1) * V_DIM)
            on = acc_ref[r0:r0 + 2 * T, own] / l_ref[r0:r0 + 2 * T, :]
            dh = on[:T] - lam * on[T:]
            o_ref[:, own] = _subln(dh, g_ref[...], lambda_init).astype(o_ref.dtype)


def _attn_sample(q, k_new, v_new, cache_k, cache_v, layer, page_table, lq1, lk1, lq2, lk2, g,
                 lambda_init, pp):
    DB, T, _ = q.shape
    n_pages = page_table.shape[1]
    assert n_pages % pp == 0 and pp % 2 == 0 and T == SUBLANES and T * N_HEADS <= LANES
    n_rows = N_HEADS * 2 * T
    pt_flat = page_table.reshape(-1)
    seq = pl.BlockSpec((None, T, D_MODEL), lambda b, p, pt: (b, 0, 0))
    new = pl.BlockSpec((None, T * N_HEADS, V_DIM), lambda b, p, pt: (b, 0, 0))
    lvec = pl.BlockSpec((1, HEAD_DIM), lambda b, p, pt: (0, 0))

    def page_spec(i):
        return pl.BlockSpec((None, None, PAGE_SIZE * N_HEADS, V_DIM),
                            lambda b, p, pt: (layer, pt[b * n_pages + p * pp + i], 0, 0))

    grid_spec = pltpu.PrefetchScalarGridSpec(
        num_scalar_prefetch=1,
        grid=(DB, n_pages // pp),
        in_specs=[seq, new, new, lvec, lvec, lvec, lvec,
                  pl.BlockSpec((1, V_DIM), lambda b, p, pt: (0, 0))]
                 + [page_spec(i) for i in range(pp)] * 2,
        out_specs=seq,
        scratch_shapes=[
            pltpu.VMEM((n_rows, D_MODEL), BF16),
            pltpu.VMEM((n_rows, 1), F32),
            pltpu.VMEM((n_rows, 1), F32),
            pltpu.VMEM((n_rows, D_MODEL), F32),
        ],
    )
    return pl.pallas_call(
        functools.partial(_attn_s_kernel, pp=pp, T=T, lambda_init=lambda_init),
        grid_spec=grid_spec,
        out_shape=jax.ShapeDtypeStruct((DB, T, D_MODEL), BF16),
        compiler_params=_params(("parallel", "arbitrary")),
        name="attn_sample",
    )(pt_flat, q, k_new, v_new, lq1, lk1, lq2, lk2, g, *([cache_k] * pp), *([cache_v] * pp))


def _tail_kernel(x_ref, orn_ref, oat_ref, g1_ref, wgr_ref, wga_ref, wbr_ref, wba_ref, wo_ref,
                 g2_ref, wu_ref, wd_ref, gf_ref, o_ref, *, final):
    x = x_ref[...]
    xn = _rmsnorm_bf16(x, g1_ref[...])
    y = jax.nn.sigmoid(jnp.dot(xn, wgr_ref[...], preferred_element_type=F32)) * \
        jnp.dot(orn_ref[...], wbr_ref[...], preferred_element_type=F32)
    y = y + jax.nn.sigmoid(jnp.dot(xn, wga_ref[...], preferred_element_type=F32)) * \
        jnp.dot(oat_ref[...], wba_ref[...], preferred_element_type=F32)
    acc = x + jnp.dot(y.astype(BF16), wo_ref[...], preferred_element_type=F32)
    xn2 = _rmsnorm_bf16(acc, g2_ref[...])
    for c in range(D_FF // D_MODEL):
        cols = slice(c * D_MODEL, (c + 1) * D_MODEL)
        hid = jnp.maximum(jnp.dot(xn2, wu_ref[:, cols], preferred_element_type=F32), 0.0)
        acc = acc + jnp.dot((hid * hid).astype(BF16), wd_ref[cols, :], preferred_element_type=F32)
    if final:
        ms = jnp.mean(acc * acc, axis=-1, keepdims=True)
        acc = acc * lax.rsqrt(ms + NORM_EPS) * gf_ref[...]
    o_ref[...] = acc


def _tail(x, o_rnn, o_attn, g1, w_in, w_branch, wo, g2, wu, wd, gf, final, tm):
    n = x.shape[0]
    row = pl.BlockSpec((tm, D_MODEL), lambda i: (i, 0))
    vec = _const_spec((1, D_MODEL))
    w = _const_spec((D_MODEL, D_MODEL))
    return pl.pallas_call(
        functools.partial(_tail_kernel, final=final),
        grid=(n // tm,),
        in_specs=[row, row, row, vec, _in_segment_spec(SEG_GRNN), _in_segment_spec(SEG_GATTN),
                  _const_spec((D_MODEL, D_MODEL), (0, 0)), _const_spec((D_MODEL, D_MODEL), (1, 0)), w, vec,
                  _const_spec((D_MODEL, D_FF)), _const_spec((D_FF, D_MODEL)), vec],
        out_specs=row,
        out_shape=jax.ShapeDtypeStruct((n, D_MODEL), F32),
        compiler_params=_params(("parallel",)),
        name="tail",
    )(x, o_rnn, o_attn, g1, w_in, w_in, w_branch, w_branch, wo, g2, wu, wd, gf)


def _rope_tables(pos):
    half = HEAD_DIM // 2
    inv = ROPE_THETA ** (-2.0 * jnp.arange(half, dtype=F32) / HEAD_DIM)
    ang = pos.astype(F32)[:, None] * inv[None, :]
    cos = jnp.cos(ang)
    sin = jnp.sin(ang)
    reps = LANES // HEAD_DIM
    return (jnp.concatenate([cos, cos] * reps, axis=1),
            jnp.concatenate([-sin, sin] * reps, axis=1))


def _gate_tiles(w):
    per = MXU_DIM // RG_BLOCK
    w5 = w.reshape(RG_GROUPS, per, RG_BLOCK, RG_BLOCK)
    eye = jnp.eye(per, dtype=w.dtype)
    t = w5[:, :, :, None, :] * eye[None, :, None, :, None]
    return t.reshape(RG_GROUPS, MXU_DIM, MXU_DIM).astype(BF16)


def kernel(x_prompt, x_sample, cache_k, cache_v, state_h, state_conv, page_table, ln1_g, w_in, conv_w, conv_b, rg_wa, rg_ba, rg_wx, rg_bx, rg_lambda, lam_q1, lam_k1, lam_q2, lam_k2, subln_g, w_branch, w_out, ln2_g, w_up, w_down, final_g):
    B, S, _ = x_prompt.shape
    DB, T, _ = x_sample.shape
    depth = w_in.shape[0]
    n_pool = cache_k.shape[1]
    past_len = page_table.shape[1] * PAGE_SIZE

    cos_p, sin_p = _rope_tables(jnp.arange(S))
    cos_s, sin_s = _rope_tables(past_len + jnp.arange(DB * T) % T)
    ck = cache_k.reshape(depth, n_pool, PAGE_SIZE * N_HEADS, V_DIM)
    cv = cache_v.reshape(depth, n_pool, PAGE_SIZE * N_HEADS, V_DIM)

    xp = x_prompt.reshape(B * S, D_MODEL)
    xs = x_sample.reshape(DB * T, D_MODEL)
    h0_p = jnp.zeros((B, 1, D_MODEL), F32)
    c0_p = jnp.zeros((B, CONV_W - 1, D_MODEL), F32)
    gf = final_g.reshape(1, D_MODEL)

    outs = [[] for _ in range(4)]
    kv_p = kv_s = None
    for l in range(depth):
        lambda_init = 0.8 - 0.6 * math.exp(-0.3 * l)
        w_in_bf = w_in[l].astype(BF16)
        w_branch_bf = w_branch[l].astype(BF16)
        wo = w_out[l].astype(BF16)
        wu = w_up[l].astype(BF16)
        wd = w_down[l].astype(BF16)
        wa4 = _gate_tiles(rg_wa[l])
        wx4 = _gate_tiles(rg_wx[l])
        vec = lambda a: a[l].reshape(1, -1)
        lam_vecs = (vec(lam_q1), vec(lam_k1), vec(lam_q2), vec(lam_k2))
        final = l == depth - 1

        def layer(x, n_seq, seq_len, cos_t, sin_t, h0, c0, tm, t_rnn, prompt, kv_all):
            qkv = _qkv(x, vec(ln1_g), w_in_bf, cos_t, sin_t, tm, prompt, l, depth, kv_all)
            kv_all = qkv[-2:]
            o_rnn, h_last, c_last = _rnn(x.reshape(n_seq, seq_len, D_MODEL), vec(ln1_g), w_in_bf, h0, c0,
                                         conv_w[l], vec(conv_b), wa4, vec(rg_ba), wx4, vec(rg_bx),
                                         vec(rg_lambda), t_rnn)
            if prompt:
                qt, ka, vt = qkv[:3]
                o_attn = _attn_prompt(qt, ka.reshape(n_seq, seq_len, D_MODEL), vt, *lam_vecs, vec(subln_g),
                                      lambda_init, ATTN_TQ, ATTN_TK, ATTN_HEADS_PER_STEP)
            else:
                new_rows = lambda a: a[l].reshape(n_seq, seq_len * N_HEADS, V_DIM)
                o_attn = _attn_sample(qkv[0].reshape(n_seq, seq_len, D_MODEL), new_rows(kv_all[0]),
                                      new_rows(kv_all[1]), ck, cv, l, page_table, *lam_vecs, vec(subln_g),
                                      lambda_init, PAGES_PER_STEP)
            x2 = _tail(x, o_rnn.reshape(-1, D_MODEL), o_attn.reshape(-1, D_MODEL), vec(ln1_g),
                       w_in_bf, w_branch_bf, wo, vec(ln2_g), wu, wd, gf, final, tm)
            return x2, kv_all, h_last.reshape(n_seq, D_MODEL), c_last

        xp, kv_p, hp, cp = layer(xp, B, S, cos_p, sin_p, h0_p, c0_p, ROW_TILE, RNN_TILE, True, kv_p)
        xs, kv_s, hs, cs = layer(xs, DB, T, cos_s, sin_s, state_h[l].reshape(DB, 1, D_MODEL),
                                 state_conv[l], DB * T, T, False, kv_s)
        for lst, val in zip(outs, (hp, cp, hs, cs)):
            lst.append(val)

    h_p, c_p, h_s, c_s = (jnp.stack(o) for o in outs)
    as_cache = lambda a, n_seq, seq_len: a.reshape(depth, n_seq, seq_len, N_HEADS, V_DIM)
    return (xp.reshape(B, S, D_MODEL), xs.reshape(DB, T, D_MODEL),
            as_cache(kv_p[0], B, S), as_cache(kv_p[1], B, S), h_p, c_p,
            as_cache(kv_s[0], DB, T), as_cache(kv_s[1], DB, T), h_s, c_s)
```

```python
import functools
import math

import jax
import jax.numpy as jnp
from jax import lax
from jax.experimental import pallas as pl
from jax.experimental.pallas import tpu as pltpu

F32 = jnp.float32
BF16 = jnp.bfloat16

D_MODEL = 1024
N_HEADS = 8
HEAD_DIM = 64
V_DIM = 2 * HEAD_DIM
PAGE_SIZE = 128
ROPE_THETA = 10000.0
SUBLN_EPS = 1e-5
N_RG_BLOCKS = 16
RG_BLOCK = D_MODEL // N_RG_BLOCKS
CONV_W = 4
RG_C = 8.0
D_FF = 4 * D_MODEL
NORM_EPS = 1e-6
ATTN_SCALE = HEAD_DIM ** -0.5
LOG2_E = math.log2(math.e)
SEG_XR, SEG_GR, SEG_Q, SEG_K, SEG_V, SEG_GRNN, SEG_GATTN = range(7)

LANES = 128
SUBLANES = 8
MXU_DIM = 256
RG_GROUPS = D_MODEL // MXU_DIM
VMEM_LIMIT = 58 * 1024 * 1024

ROW_TILE = 512
RNN_TILE = 256
ATTN_TQ, ATTN_TK = 512, 256
ATTN_HEADS_PER_STEP = 4
PAGES_PER_STEP = 16

NEG_INF = float("-inf")
_NT = (((1,), (1,)), ((), ()))


def _params(semantics):
    return pltpu.CompilerParams(dimension_semantics=semantics, vmem_limit_bytes=VMEM_LIMIT)


def _const_spec(shape, index=None):
    index = (0,) * len(shape) if index is None else index
    return pl.BlockSpec(shape, lambda *_: index, pipeline_mode=pl.Buffered(1))


def _in_segment_spec(segment):
    return _const_spec((D_MODEL, D_MODEL), (0, segment))


def _rmsnorm_bf16(x, g):
    ms = jnp.mean(x * x, axis=-1, keepdims=True)
    return (x * lax.rsqrt(ms + NORM_EPS) * g).astype(BF16)


def _diff_lambda(lq1_ref, lk1_ref, lq2_ref, lk2_ref, lambda_init):
    s1 = jnp.sum(lq1_ref[...] * lk1_ref[...], axis=-1, keepdims=True)
    s2 = jnp.sum(lq2_ref[...] * lk2_ref[...], axis=-1, keepdims=True)
    return jnp.exp(s1) - jnp.exp(s2) + lambda_init


def _subln(o, g, lambda_init):
    ms = jnp.mean(o * o, axis=-1, keepdims=True)
    return (o * lax.rsqrt(ms + SUBLN_EPS) * g) * (1.0 - lambda_init)


def _qkv_kernel(x_ref, g_ref, wq_ref, wk_ref, wv_ref, cos_ref, sin_ref, *refs, tm, transposed, n_aliased):
    out_refs = refs[n_aliased:]
    if transposed:
        q_ref, ka_ref, va_ref, ko_ref, vo_ref = out_refs
    else:
        q_ref, ko_ref, vo_ref = out_refs
    if n_aliased == 0:
        for ref in (ko_ref, vo_ref):
            if ref.shape[0] > 1:
                ref[1:] = jnp.zeros((ref.shape[0] - 1,) + ref.shape[1:], F32)
        ko_ref, vo_ref = ko_ref.at[0], vo_ref.at[0]
    xn = _rmsnorm_bf16(x_ref[...], g_ref[...])
    cos = cos_ref[...]
    sin = sin_ref[...]
    lane = lax.broadcasted_iota(jnp.int32, cos.shape, 1)
    lower = (lane & (HEAD_DIM // 2)) == 0

    def head_cols(h):
        return slice(h * V_DIM, (h + 1) * V_DIM)

    def rope_heads(z):
        for h in range(N_HEADS):
            zh = z[:, head_cols(h)]
            partner = jnp.where(lower,
                                pltpu.roll(zh, LANES - HEAD_DIM // 2, 1),
                                pltpu.roll(zh, HEAD_DIM // 2, 1))
            yield h, zh * cos + partner * sin

    zq = jnp.dot(xn, wq_ref[...], preferred_element_type=F32)
    for h, qh in rope_heads(zq):
        qh = qh * (ATTN_SCALE * LOG2_E)
        if transposed:
            q_ref[head_cols(h), :] = qh.T.astype(BF16)
        else:
            q_ref[:, head_cols(h)] = qh.astype(BF16)

    zk = jnp.dot(xn, wk_ref[...], preferred_element_type=F32)
    for h, kh in rope_heads(zk):
        ko_ref[pl.ds(h, tm, stride=N_HEADS), :] = kh
        if transposed:
            ka_ref[:, head_cols(h)] = kh.astype(BF16)

    zv = jnp.dot(xn, wv_ref[...], preferred_element_type=F32)
    for h in range(N_HEADS):
        vh = zv[:, head_cols(h)]
        vo_ref[pl.ds(h, tm, stride=N_HEADS), :] = vh
        if transposed:
            va_ref[head_cols(h), :] = vh.T.astype(BF16)


def _qkv(x, g, w_in, cos_t, sin_t, tm, transposed, layer, depth, earlier):
    n = x.shape[0]
    n_tiles = n // tm
    n_tab = cos_t.shape[0] // tm
    row = lambda i: (i, 0)
    flat = pl.BlockSpec((tm, D_MODEL), row)
    chan = pl.BlockSpec((D_MODEL, tm), lambda i: (0, i))
    if earlier is None:
        assert layer == 0
        cache = pl.BlockSpec((depth, tm * N_HEADS, V_DIM), lambda i: (0, i, 0))
    else:
        cache = pl.BlockSpec((None, tm * N_HEADS, V_DIM), lambda i: (layer, i, 0))
    tab = pl.BlockSpec((tm, LANES), lambda i: (i % n_tab, 0))
    flat_bf16 = jax.ShapeDtypeStruct((n, D_MODEL), BF16)
    chan_bf16 = jax.ShapeDtypeStruct((D_MODEL, n), BF16)
    cache_f32 = jax.ShapeDtypeStruct((depth, n * N_HEADS, V_DIM), F32)
    if transposed:
        out_specs = [chan, flat, chan, cache, cache]
        out_shape = [chan_bf16, flat_bf16, chan_bf16, cache_f32, cache_f32]
    else:
        out_specs = [flat, cache, cache]
        out_shape = [flat_bf16, cache_f32, cache_f32]
    in_specs = [flat, _const_spec((1, D_MODEL)), _in_segment_spec(SEG_Q), _in_segment_spec(SEG_K),
                _in_segment_spec(SEG_V), tab, tab]
    args = [x, g, w_in, w_in, w_in, cos_t, sin_t]
    aliases = {}
    if earlier is not None:
        n_out = len(out_shape)
        aliases = {len(args): n_out - 2, len(args) + 1: n_out - 1}
        in_specs += [pl.BlockSpec(memory_space=pl.ANY)] * 2
        args += list(earlier)
    return pl.pallas_call(
        functools.partial(_qkv_kernel, tm=tm, transposed=transposed, n_aliased=len(aliases)),
        grid=(n_tiles,),
        in_specs=in_specs,
        out_specs=out_specs,
        out_shape=out_shape,
        input_output_aliases=aliases,
        compiler_params=_params(("parallel",)),
        name="qkv",
    )(*args)


def _rnn_kernel(x_ref, g_ref, wxr_ref, wgr_ref, h0_ref, c0_ref, cw_ref, cb_ref,
                wa_ref, ba_ref, wx_ref, bx_ref, lam_ref,
                o_ref, hl_ref, cl_ref,
                xe_ref, gg_ref, al_ref, bl_ref, hin_ref, o32_ref, h_ref, *, T):
    t = pl.program_id(1)
    halo = CONV_W - 1
    G = T // SUBLANES
    NB = D_MODEL // LANES
    blocks = [(c, slice(c * LANES, (c + 1) * LANES)) for c in range(NB)]

    @pl.when(t == 0)
    def _():
        for c, cols in blocks:
            xe_ref[c, SUBLANES - halo:SUBLANES, :] = c0_ref[:, cols]
        h_ref[...] = h0_ref[...]

    @pl.when(t > 0)
    def _():
        for c, _ in blocks:
            xe_ref[c, SUBLANES - halo:SUBLANES, :] = xe_ref[c, T + SUBLANES - halo:T + SUBLANES, :]

    def class_rows(ref, c, start):
        return ref[c, pl.ds(start, G, stride=SUBLANES), :]

    xn = _rmsnorm_bf16(x_ref[...], g_ref[...])
    xr = jnp.dot(xn, wxr_ref[...], preferred_element_type=F32)
    gg = jax.nn.gelu(jnp.dot(xn, wgr_ref[...], preferred_element_type=F32))
    for c, cols in blocks:
        xe_ref[c, SUBLANES:SUBLANES + T, :] = xr[:, cols]
        gg_ref[c] = gg[:, cols]

    def class_block(v, k):
        return v[k * G:(k + 1) * G, :]

    for grp in range(RG_GROUPS):
        gcols = slice(grp * MXU_DIM, (grp + 1) * MXU_DIM)
        gblocks = blocks[grp * (MXU_DIM // LANES):(grp + 1) * (MXU_DIM // LANES)]

        classes = []
        for k in range(SUBLANES):
            parts = []
            for c, cols in gblocks:
                acc = cb_ref[:, cols]
                for j in range(CONV_W):
                    acc = acc + class_rows(xe_ref, c, SUBLANES - halo + j + k) * cw_ref[j:j + 1, cols]
                parts.append(acc)
            classes.append(jnp.concatenate(parts, axis=1))
        xc = jnp.concatenate(classes, axis=0)
        xcb = xc.astype(BF16)

        def gate(w_ref, bias_ref):
            z = jnp.dot(xcb, w_ref[grp], preferred_element_type=F32)
            return jax.nn.sigmoid(z + bias_ref[:, gcols])

        r = gate(wa_ref, ba_ref)
        i = gate(wx_ref, bx_ref)
        log_a = r * ((-RG_C) * jax.nn.softplus(-lam_ref[:, gcols]))
        a = jnp.exp(log_a)
        u = jnp.sqrt(1.0 - a * a) * (i * xc)

        a_cum = class_block(a, 0)
        b_cum = class_block(u, 0)
        al_ref[0, :, gcols] = a_cum
        bl_ref[0, :, gcols] = b_cum
        for k in range(1, SUBLANES):
            ak = class_block(a, k)
            b_cum = ak * b_cum + class_block(u, k)
            a_cum = ak * a_cum
            al_ref[k, :, gcols] = a_cum
            bl_ref[k, :, gcols] = b_cum
        h = h_ref[:, gcols]
        for gi in range(G):
            hin_ref[gi:gi + 1, gcols] = h
            h = bl_ref[SUBLANES - 1, gi:gi + 1, gcols] + al_ref[SUBLANES - 1, gi:gi + 1, gcols] * h
        h_ref[:, gcols] = h
        for c, cols in gblocks:
            h_in = hin_ref[:, cols]
            for k in range(SUBLANES):
                hk = bl_ref[k, :, cols] + al_ref[k, :, cols] * h_in
                o32_ref[c, pl.ds(k, G, stride=SUBLANES), :] = hk * class_rows(gg_ref, c, k)
            o_ref[:, cols] = o32_ref[c].astype(o_ref.dtype)
            cl_ref[:, cols] = xe_ref[c, T + SUBLANES - halo:T + SUBLANES, :]
    hl_ref[...] = h_ref[...]


def _rnn(x, g, w_in, h0, c0, cw, cb, wa4, ba, wx4, bx, lam, T):
    B, S, _ = x.shape
    assert S % T == 0 and T % SUBLANES == 0 and T >= CONV_W - 1
    G = T // SUBLANES
    NB = D_MODEL // LANES
    seq = lambda b, t: (b, t, 0)
    per_b = lambda b, t: (b, 0, 0)
    vec = _const_spec((1, D_MODEL))
    wgate = _const_spec((RG_GROUPS, MXU_DIM, MXU_DIM))
    tile = pltpu.VMEM((NB, T, LANES), F32)
    return pl.pallas_call(
        functools.partial(_rnn_kernel, T=T),
        grid=(B, S // T),
        in_specs=[
            pl.BlockSpec((None, T, D_MODEL), seq),
            vec, _in_segment_spec(SEG_XR), _in_segment_spec(SEG_GR),
            pl.BlockSpec((None, 1, D_MODEL), per_b),
            pl.BlockSpec((None, CONV_W - 1, D_MODEL), per_b),
            _const_spec((CONV_W, D_MODEL)),
            vec, wgate, vec, wgate, vec, vec,
        ],
        out_specs=[
            pl.BlockSpec((None, T, D_MODEL), seq),
            pl.BlockSpec((None, 1, D_MODEL), per_b),
            pl.BlockSpec((None, CONV_W - 1, D_MODEL), per_b),
        ],
        out_shape=[
            jax.ShapeDtypeStruct((B, S, D_MODEL), BF16),
            jax.ShapeDtypeStruct((B, 1, D_MODEL), F32),
            jax.ShapeDtypeStruct((B, CONV_W - 1, D_MODEL), F32),
        ],
        scratch_shapes=[
            pltpu.VMEM((NB, T + SUBLANES, LANES), F32),
            tile,
            pltpu.VMEM((SUBLANES, G, D_MODEL), F32),
            pltpu.VMEM((SUBLANES, G, D_MODEL), F32),
            pltpu.VMEM((G, D_MODEL), F32),
            tile,
            pltpu.VMEM((1, D_MODEL), F32),
        ],
        compiler_params=_params(("parallel", "arbitrary")),
        name="rnn",
    )(x, g, w_in, w_in, h0, c0, cw, cb, wa4, ba, wx4, bx, lam)


def _attn_p_kernel(qt_ref, k_ref, vt_ref, bias_ref, lq1_ref, lk1_ref, lq2_ref, lk2_ref, g_ref,
                   o_ref, q2_ref, s_ref, mx_ref, p_ref, alpha_ref, m_ref, acc_ref,
                   *, tq, tk, hps, lambda_init):
    qi = pl.program_id(2)
    n_pairs = qi
    heads = range(hps)
    ones = jnp.ones((acc_ref.shape[1] - V_DIM, tk), BF16)
    every = slice(None)
    late = slice(tq, 2 * tq)

    def head_rows(h):
        return slice(h * V_DIM, (h + 1) * V_DIM)

    row = lax.broadcasted_iota(jnp.int32, (V_DIM, tk), 0)
    for h in heads:
        for half in range(2):
            qt = qt_ref[head_rows(h), half * tk:(half + 1) * tk]
            zero = jnp.zeros_like(qt)
            q2_ref[h, :, (2 * half) * tk:(2 * half + 1) * tk] = jnp.where(row < HEAD_DIM, qt, zero)
            q2_ref[h, :, (2 * half + 1) * tk:(2 * half + 2) * tk] = jnp.where(row >= HEAD_DIM, qt, zero)
    m_ref[...] = jnp.full(m_ref.shape, NEG_INF, F32)
    acc_ref[...] = jnp.zeros(acc_ref.shape, F32)
    for h in heads:
        p_ref[h, 1] = jnp.zeros(p_ref.shape[2:], BF16)
        alpha_ref[h, 1] = jnp.ones(alpha_ref.shape[2:], F32)

    def produce(h, kc, slot, diag, cols=every):
        r0 = pl.multiple_of(kc * tk, tk)
        s = jnp.dot(k_ref[pl.ds(r0, tk), head_rows(h)], q2_ref[h, :, cols], preferred_element_type=F32)
        if diag is not None:
            s = s + bias_ref[diag, :, cols]
        s_ref[h, slot, :, cols] = s
        mx_ref[h, slot, :, cols] = jnp.max(s, axis=0, keepdims=True)

    def apply_values(h, kc, slot, cols=every):
        r0 = pl.multiple_of(kc * tk, tk)
        vt1 = jnp.concatenate([vt_ref[head_rows(h), pl.ds(r0, tk)], ones], axis=0)
        pv = jnp.dot(vt1, p_ref[h, slot, :, cols], preferred_element_type=F32)
        acc_ref[h, :, cols] = alpha_ref[h, slot, :, cols] * acc_ref[h, :, cols] + pv

    def softmax(h, slot, cols=every):
        m_old = m_ref[h, :, cols]
        m_new = jnp.maximum(m_old, mx_ref[h, slot, :, cols])
        m_ref[h, :, cols] = m_new
        alpha_ref[h, slot, :, cols] = jnp.exp2(m_old - m_new)
        p_ref[h, slot, :, cols] = jnp.exp2(s_ref[h, slot, :, cols] - m_new).astype(BF16)

    def step(kc, slot, next_diag, next_cols=every, cols=every):
        if next_diag is not False:
            for h in heads:
                produce(h, kc + 1, 1 - slot, next_diag, next_cols)
        for h in heads:
            apply_values(h, jnp.maximum(kc - 1, 0), 1 - slot)
        for h in heads:
            softmax(h, slot, cols)

    def pair(j, then_diag):
        step(2 * j, 0, None)
        step(2 * j + 1, 1, then_diag)

    @pl.when(n_pairs == 0)
    def _():
        for h in heads:
            produce(h, 0, 0, 0)

    @pl.when(n_pairs > 0)
    def _():
        for h in heads:
            produce(h, 0, 0, None)

    def body(j, carry):
        pair(j, None)
        return carry

    lax.fori_loop(0, n_pairs - 1, body, 0)

    @pl.when(n_pairs > 0)
    def _():
        pair(n_pairs - 1, 0)

    step(2 * n_pairs, 0, 1, next_cols=late)
    step(2 * n_pairs + 1, 1, False, cols=late)
    lam = _diff_lambda(lq1_ref, lk1_ref, lq2_ref, lk2_ref, lambda_init)
    for h in heads:
        apply_values(h, 2 * n_pairs + 1, 1, late)
        on = acc_ref[h, :V_DIM] / acc_ref[h, V_DIM:V_DIM + 1]
        ot = jnp.concatenate([on[:, 2 * half * tk:(2 * half + 1) * tk]
                              - lam * on[:, (2 * half + 1) * tk:(2 * half + 2) * tk] for half in range(2)], axis=1)
        o_ref[:, head_rows(h)] = _subln(ot.T, g_ref[...], lambda_init).astype(o_ref.dtype)


def _attn_prompt(qt, k, vt, lq1, lk1, lq2, lk2, g, lambda_init, tq, tk, hps):
    B, S, _ = k.shape
    assert S % tq == 0 and tq == 2 * tk and tq & (tq - 1) == 0 and N_HEADS % hps == 0
    nq = S // tq
    lvec = _const_spec((1, HEAD_DIM))
    slots = lambda shape, dtype: pltpu.VMEM((hps, 2) + shape, dtype)
    key = lax.broadcasted_iota(jnp.int32, (2, tk, 2 * tq), 0) * tk + lax.broadcasted_iota(jnp.int32, (2, tk, 2 * tq), 1)
    col = lax.broadcasted_iota(jnp.int32, (2, tk, 2 * tq), 2)
    qry = (col // tq) * tk + (col & (tk - 1))
    bias = jnp.where(key <= qry, 0.0, NEG_INF).astype(F32)
    ones_rows = 2 * SUBLANES
    return pl.pallas_call(
        functools.partial(_attn_p_kernel, tq=tq, tk=tk, hps=hps, lambda_init=lambda_init),
        grid=(B, N_HEADS // hps, nq),
        in_specs=[
            pl.BlockSpec((hps * V_DIM, tq), lambda b, h, i: (h, b * nq + i)),
            pl.BlockSpec((None, S, hps * V_DIM), lambda b, h, i: (b, 0, h)),
            pl.BlockSpec((hps * V_DIM, S), lambda b, h, i: (h, b)),
            _const_spec((2, tk, 2 * tq)),
            lvec, lvec, lvec, lvec,
            _const_spec((1, V_DIM)),
        ],
        out_specs=pl.BlockSpec((None, tq, hps * V_DIM), lambda b, h, i: (b, i, h)),
        out_shape=jax.ShapeDtypeStruct((B, S, D_MODEL), BF16),
        scratch_shapes=[
            pltpu.VMEM((hps, V_DIM, 2 * tq), BF16),
            slots((tk, 2 * tq), F32),
            slots((1, 2 * tq), F32),
            slots((tk, 2 * tq), BF16),
            slots((1, 2 * tq), F32),
            pltpu.VMEM((hps, 1, 2 * tq), F32),
            pltpu.VMEM((hps, V_DIM + ones_rows, 2 * tq), F32),
        ],
        compiler_params=_params(("parallel", "parallel", "arbitrary")),
        name="attn_prompt",
    )(qt, k, vt, bias, lq1, lk1, lq2, lk2, g)


def _attn_s_kernel(pt_ref, q_ref, kn_ref, vn_ref, lq1_ref, lk1_ref, lq2_ref, lk2_ref, g_ref, *rest,
                   pp, T, lambda_init):
    k_refs = rest[:pp]
    v_refs = rest[pp:2 * pp]
    o_ref, qm_ref, m_ref, l_ref, acc_ref = rest[2 * pp:]
    del pt_ref
    p = pl.program_id(1)
    n_rows = N_HEADS * 2 * T

    def token_major(ref, n_tokens):
        return jnp.concatenate([ref[pl.ds(h, n_tokens, stride=N_HEADS), :] for h in range(N_HEADS)], axis=1)

    @pl.when(p == 0)
    def _():
        q = q_ref[...].astype(F32)
        qrep = jnp.concatenate([q] * (2 * N_HEADS), axis=0)
        rr = lax.broadcasted_iota(jnp.int32, qrep.shape, 0)
        cc = lax.broadcasted_iota(jnp.int32, qrep.shape, 1)
        qm_ref[...] = jnp.where(cc // HEAD_DIM == rr // T, qrep, 0.0).astype(BF16)
        m_ref[...] = jnp.full(m_ref.shape, NEG_INF, F32)
        l_ref[...] = jnp.zeros(l_ref.shape, F32)
        acc_ref[...] = jnp.zeros(acc_ref.shape, F32)

    qm = qm_ref[...]

    def scores(kb):
        return lax.dot_general(qm, kb, _NT, preferred_element_type=F32)

    def fold(s, v_list):
        width = s.shape[1] // len(v_list)
        m_old = m_ref[...]
        m_new = jnp.maximum(m_old, jnp.max(s, axis=-1, keepdims=True))
        alpha = jnp.exp2(m_old - m_new)
        pb = jnp.exp2(s - m_new)
        l_ref[...] = alpha * l_ref[...] + jnp.sum(pb, axis=-1, keepdims=True)
        pb = pb.astype(BF16)
        pv = None
        for i, vb in enumerate(v_list):
            part = jnp.dot(pb[:, i * width:(i + 1) * width], vb, preferred_element_type=F32)
            pv = part if pv is None else pv + part
        acc_ref[...] = alpha * acc_ref[...] + pv
        m_ref[...] = m_new

    s_pages = [scores(token_major(k_refs[i], PAGE_SIZE).astype(BF16)) for i in range(pp)]
    for a in range(0, pp, pp // 2):
        group = range(a, a + pp // 2)
        fold(jnp.concatenate([s_pages[i] for i in group], axis=1),
             [token_major(v_refs[i], PAGE_SIZE).astype(BF16) for i in group])

    @pl.when(p == pl.num_programs(1) - 1)
    def _():
        pad = jnp.zeros((LANES - T, D_MODEL), F32)
        kn = jnp.concatenate([token_major(kn_ref, T), pad], axis=0).astype(BF16)
        vn = jnp.concatenate([token_major(vn_ref, T), pad], axis=0).astype(BF16)
        s = scores(kn)
        qry = lax.broadcasted_iota(jnp.int32, s.shape, 0) % T
        key = lax.broadcasted_iota(jnp.int32, s.shape, 1)
        fold(jnp.where(key <= qry, s, NEG_INF), [vn])

        lam = _diff_lambda(lq1_ref, lk1_ref, lq2_ref, lk2_ref, lambda_init)
        for h in range(N_HEADS):
            r0 = h * 2 * T
            own = slice(h * V_DIM, (h + 1) * V_DIM)
            on = acc_ref[r0:r0 + 2 * T, own] / l_ref[r0:r0 + 2 * T, :]
            dh = on[:T] - lam * on[T:]
            o_ref[:, own] = _subln(dh, g_ref[...], lambda_init).astype(o_ref.dtype)


def _attn_sample(q, k_new, v_new, cache_k, cache_v, layer, page_table, lq1, lk1, lq2, lk2, g,
                 lambda_init, pp):
    DB, T, _ = q.shape
    n_pages = page_table.shape[1]
    assert n_pages % pp == 0 and pp % 2 == 0 and T == SUBLANES and T * N_HEADS <= LANES
    n_rows = N_HEADS * 2 * T
    pt_flat = page_table.reshape(-1)
    seq = pl.BlockSpec((None, T, D_MODEL), lambda b, p, pt: (b, 0, 0))
    new = pl.BlockSpec((None, T * N_HEADS, V_DIM), lambda b, p, pt: (b, 0, 0))
    lvec = pl.BlockSpec((1, HEAD_DIM), lambda b, p, pt: (0, 0))

    def page_spec(i):
        return pl.BlockSpec((None, None, PAGE_SIZE * N_HEADS, V_DIM),
                            lambda b, p, pt: (layer, pt[b * n_pages + p * pp + i], 0, 0))

    grid_spec = pltpu.PrefetchScalarGridSpec(
        num_scalar_prefetch=1,
        grid=(DB, n_pages // pp),
        in_specs=[seq, new, new, lvec, lvec, lvec, lvec,
                  pl.BlockSpec((1, V_DIM), lambda b, p, pt: (0, 0))]
                 + [page_spec(i) for i in range(pp)] * 2,
        out_specs=seq,
        scratch_shapes=[
            pltpu.VMEM((n_rows, D_MODEL), BF16),
            pltpu.VMEM((n_rows, 1), F32),
            pltpu.VMEM((n_rows, 1), F32),
            pltpu.VMEM((n_rows, D_MODEL), F32),
        ],
    )
    return pl.pallas_call(
        functools.partial(_attn_s_kernel, pp=pp, T=T, lambda_init=lambda_init),
        grid_spec=grid_spec,
        out_shape=jax.ShapeDtypeStruct((DB, T, D_MODEL), BF16),
        compiler_params=_params(("parallel", "arbitrary")),
        name="attn_sample",
    )(pt_flat, q, k_new, v_new, lq1, lk1, lq2, lk2, g, *([cache_k] * pp), *([cache_v] * pp))


def _tail_kernel(x_ref, orn_ref, oat_ref, g1_ref, wgr_ref, wga_ref, wbr_ref, wba_ref, wo_ref,
                 g2_ref, wu_ref, wd_ref, gf_ref, o_ref, *, final):
    x = x_ref[...]
    xn = _rmsnorm_bf16(x, g1_ref[...])
    y = jax.nn.sigmoid(jnp.dot(xn, wgr_ref[...], preferred_element_type=F32)) * \
        jnp.dot(orn_ref[...], wbr_ref[...], preferred_element_type=F32)
    y = y + jax.nn.sigmoid(jnp.dot(xn, wga_ref[...], preferred_element_type=F32)) * \
        jnp.dot(oat_ref[...], wba_ref[...], preferred_element_type=F32)
    acc = x + jnp.dot(y.astype(BF16), wo_ref[...], preferred_element_type=F32)
    xn2 = _rmsnorm_bf16(acc, g2_ref[...])
    for c in range(D_FF // D_MODEL):
        cols = slice(c * D_MODEL, (c + 1) * D_MODEL)
        hid = jnp.maximum(jnp.dot(xn2, wu_ref[:, cols], preferred_element_type=F32), 0.0)
        acc = acc + jnp.dot((hid * hid).astype(BF16), wd_ref[cols, :], preferred_element_type=F32)
    if final:
        ms = jnp.mean(acc * acc, axis=-1, keepdims=True)
        acc = acc * lax.rsqrt(ms + NORM_EPS) * gf_ref[...]
    o_ref[...] = acc


def _tail(x, o_rnn, o_attn, g1, w_in, w_branch, wo, g2, wu, wd, gf, final, tm):
    n = x.shape[0]
    row = pl.BlockSpec((tm, D_MODEL), lambda i: (i, 0))
    vec = _const_spec((1, D_MODEL))
    w = _const_spec((D_MODEL, D_MODEL))
    return pl.pallas_call(
        functools.partial(_tail_kernel, final=final),
        grid=(n // tm,),
        in_specs=[row, row, row, vec, _in_segment_spec(SEG_GRNN), _in_segment_spec(SEG_GATTN),
                  _const_spec((D_MODEL, D_MODEL), (0, 0)), _const_spec((D_MODEL, D_MODEL), (1, 0)), w, vec,
                  _const_spec((D_MODEL, D_FF)), _const_spec((D_FF, D_MODEL)), vec],
        out_specs=row,
        out_shape=jax.ShapeDtypeStruct((n, D_MODEL), F32),
        compiler_params=_params(("parallel",)),
        name="tail",
    )(x, o_rnn, o_attn, g1, w_in, w_in, w_branch, w_branch, wo, g2, wu, wd, gf)


def _rope_tables(pos):
    half = HEAD_DIM // 2
    inv = ROPE_THETA ** (-2.0 * jnp.arange(half, dtype=F32) / HEAD_DIM)
    ang = pos.astype(F32)[:, None] * inv[None, :]
    cos = jnp.cos(ang)
    sin = jnp.sin(ang)
    reps = LANES // HEAD_DIM
    return (jnp.concatenate([cos, cos] * reps, axis=1),
            jnp.concatenate([-sin, sin] * reps, axis=1))


def _gate_tiles(w):
    per = MXU_DIM // RG_BLOCK
    w5 = w.reshape(RG_GROUPS, per, RG_BLOCK, RG_BLOCK)
    eye = jnp.eye(per, dtype=w.dtype)
    t = w5[:, :, :, None, :] * eye[None, :, None, :, None]
    return t.reshape(RG_GROUPS, MXU_DIM, MXU_DIM).astype(BF16)


def kernel(x_prompt, x_sample, cache_k, cache_v, state_h, state_conv, page_table, ln1_g, w_in, conv_w, conv_b, rg_wa, rg_ba, rg_wx, rg_bx, rg_lambda, lam_q1, lam_k1, lam_q2, lam_k2, subln_g, w_branch, w_out, ln2_g, w_up, w_down, final_g):
    B, S, _ = x_prompt.shape
    DB, T, _ = x_sample.shape
    depth = w_in.shape[0]
    n_pool = cache_k.shape[1]
    past_len = page_table.shape[1] * PAGE_SIZE

    cos_p, sin_p = _rope_tables(jnp.arange(S))
    cos_s, sin_s = _rope_tables(past_len + jnp.arange(DB * T) % T)
    ck = cache_k.reshape(depth, n_pool, PAGE_SIZE * N_HEADS, V_DIM)
    cv = cache_v.reshape(depth, n_pool, PAGE_SIZE * N_HEADS, V_DIM)

    xp = x_prompt.reshape(B * S, D_MODEL)
    xs = x_sample.reshape(DB * T, D_MODEL)
    h0_p = jnp.zeros((B, 1, D_MODEL), F32)
    c0_p = jnp.zeros((B, CONV_W - 1, D_MODEL), F32)
    gf = final_g.reshape(1, D_MODEL)

    outs = [[] for _ in range(4)]
    kv_p = kv_s = None
    for l in range(depth):
        lambda_init = 0.8 - 0.6 * math.exp(-0.3 * l)
        w_in_bf = w_in[l].astype(BF16)
        w_branch_bf = w_branch[l].astype(BF16)
        wo = w_out[l].astype(BF16)
        wu = w_up[l].astype(BF16)
        wd = w_down[l].astype(BF16)
        wa4 = _gate_tiles(rg_wa[l])
        wx4 = _gate_tiles(rg_wx[l])
        vec = lambda a: a[l].reshape(1, -1)
        lam_vecs = (vec(lam_q1), vec(lam_k1), vec(lam_q2), vec(lam_k2))
        final = l == depth - 1

        def layer(x, n_seq, seq_len, cos_t, sin_t, h0, c0, tm, t_rnn, prompt, kv_all):
            qkv = _qkv(x, vec(ln1_g), w_in_bf, cos_t, sin_t, tm, prompt, l, depth, kv_all)
            kv_all = qkv[-2:]
            o_rnn, h_last, c_last = _rnn(x.reshape(n_seq, seq_len, D_MODEL), vec(ln1_g), w_in_bf, h0, c0,
                                         conv_w[l], vec(conv_b), wa4, vec(rg_ba), wx4, vec(rg_bx),
                                         vec(rg_lambda), t_rnn)
            if prompt:
                qt, ka, vt = qkv[:3]
                o_attn = _attn_prompt(qt, ka.reshape(n_seq, seq_len, D_MODEL), vt, *lam_vecs, vec(subln_g),
                                      lambda_init, ATTN_TQ, ATTN_TK, ATTN_HEADS_PER_STEP)
            else:
                new_rows = lambda a: a[l].reshape(n_seq, seq_len * N_HEADS, V_DIM)
                o_attn = _attn_sample(qkv[0].reshape(n_seq, seq_len, D_MODEL), new_rows(kv_all[0]),
                                      new_rows(kv_all[1]), ck, cv, l, page_table, *lam_vecs, vec(subln_g),
                                      lambda_init, PAGES_PER_STEP)
            x2 = _tail(x, o_rnn.reshape(-1, D_MODEL), o_attn.reshape(-1, D_MODEL), vec(ln1_g),
                       w_in_bf, w_branch_bf, wo, vec(ln2_g), wu, wd, gf, final, tm)
            return x2, kv_all, h_last.reshape(n_seq, D_MODEL), c_last

        xp, kv_p, hp, cp = layer(xp, B, S, cos_p, sin_p, h0_p, c0_p, ROW_TILE, RNN_TILE, True, kv_p)
        xs, kv_s, hs, cs = layer(xs, DB, T, cos_s, sin_s, state_h[l].reshape(DB, 1, D_MODEL),
                                 state_conv[l], DB * T, T, False, kv_s)
        for lst, val in zip(outs, (hp, cp, hs, cs)):
            lst.append(val)

    h_p, c_p, h_s, c_s = (jnp.stack(o) for o in outs)
    as_cache = lambda a, n_seq, seq_len: a.reshape(depth, n_seq, seq_len, N_HEADS, V_DIM)
    return (xp.reshape(B, S, D_MODEL), xs.reshape(DB, T, D_MODEL),
            as_cache(kv_p[0], B, S), as_cache(kv_p[1], B, S), h_p, c_p,
            as_cache(kv_s[0], DB, T), as_cache(kv_s[1], DB, T), h_s, c_s)
```

```python
import functools
import math

import jax
import jax.numpy as jnp
from jax import lax
from jax.experimental import pallas as pl
from jax.experimental.pallas import tpu as pltpu

F32 = jnp.float32
BF16 = jnp.bfloat16

D_MODEL = 1024
N_HEADS = 8
HEAD_DIM = 64
V_DIM = 2 * HEAD_DIM
PAGE_SIZE = 128
ROPE_THETA = 10000.0
SUBLN_EPS = 1e-5
N_RG_BLOCKS = 16
RG_BLOCK = D_MODEL // N_RG_BLOCKS
CONV_W = 4
RG_C = 8.0
D_FF = 4 * D_MODEL
NORM_EPS = 1e-6
ATTN_SCALE = HEAD_DIM ** -0.5
LOG2_E = math.log2(math.e)
SEG_XR, SEG_GR, SEG_Q, SEG_K, SEG_V, SEG_GRNN, SEG_GATTN = range(7)

LANES = 128
SUBLANES = 8
MXU_DIM = 256
RG_GROUPS = D_MODEL // MXU_DIM
VMEM_LIMIT = 58 * 1024 * 1024

ROW_TILE = 512
RNN_TILE = 256
ATTN_TQ, ATTN_TK = 512, 256
ATTN_HEADS_PER_STEP = 4
PAGES_PER_STEP = 16

NEG_INF = float("-inf")
_NT = (((1,), (1,)), ((), ()))


def _params(semantics):
    return pltpu.CompilerParams(dimension_semantics=semantics, vmem_limit_bytes=VMEM_LIMIT)


def _const_spec(shape, index=None):
    index = (0,) * len(shape) if index is None else index
    return pl.BlockSpec(shape, lambda *_: index, pipeline_mode=pl.Buffered(1))


def _in_segment_spec(segment):
    return _const_spec((D_MODEL, D_MODEL), (0, segment))


def _rmsnorm_bf16(x, g):
    ms = jnp.mean(x * x, axis=-1, keepdims=True)
    return (x * lax.rsqrt(ms + NORM_EPS) * g).astype(BF16)


def _diff_lambda(lq1_ref, lk1_ref, lq2_ref, lk2_ref, lambda_init):
    s1 = jnp.sum(lq1_ref[...] * lk1_ref[...], axis=-1, keepdims=True)
    s2 = jnp.sum(lq2_ref[...] * lk2_ref[...], axis=-1, keepdims=True)
    return jnp.exp(s1) - jnp.exp(s2) + lambda_init


def _subln(o, g, lambda_init):
    ms = jnp.mean(o * o, axis=-1, keepdims=True)
    return (o * lax.rsqrt(ms + SUBLN_EPS) * g) * (1.0 - lambda_init)


def _qkv_kernel(x_ref, g_ref, wq_ref, wk_ref, wv_ref, cos_ref, sin_ref, *refs, tm, transposed, n_aliased):
    out_refs = refs[n_aliased:]
    if transposed:
        q_ref, ka_ref, va_ref, ko_ref, vo_ref = out_refs
    else:
        q_ref, ko_ref, vo_ref = out_refs
    if n_aliased == 0:
        for ref in (ko_ref, vo_ref):
            if ref.shape[0] > 1:
                ref[1:] = jnp.zeros((ref.shape[0] - 1,) + ref.shape[1:], F32)
        ko_ref, vo_ref = ko_ref.at[0], vo_ref.at[0]
    xn = _rmsnorm_bf16(x_ref[...], g_ref[...])
    cos = cos_ref[...]
    sin = sin_ref[...]
    lane = lax.broadcasted_iota(jnp.int32, cos.shape, 1)
    lower = (lane & (HEAD_DIM // 2)) == 0

    def head_cols(h):
        return slice(h * V_DIM, (h + 1) * V_DIM)

    def rope_heads(z):
        for h in range(N_HEADS):
            zh = z[:, head_cols(h)]
            partner = jnp.where(lower,
                                pltpu.roll(zh, LANES - HEAD_DIM // 2, 1),
                                pltpu.roll(zh, HEAD_DIM // 2, 1))
            yield h, zh * cos + partner * sin

    zq = jnp.dot(xn, wq_ref[...], preferred_element_type=F32)
    for h, qh in rope_heads(zq):
        qh = qh * (ATTN_SCALE * LOG2_E)
        if transposed:
            q_ref[head_cols(h), :] = qh.T.astype(BF16)
        else:
            q_ref[:, head_cols(h)] = qh.astype(BF16)

    zk = jnp.dot(xn, wk_ref[...], preferred_element_type=F32)
    for h, kh in rope_heads(zk):
        ko_ref[pl.ds(h, tm, stride=N_HEADS), :] = kh
        if transposed:
            ka_ref[:, head_cols(h)] = kh.astype(BF16)

    zv = jnp.dot(xn, wv_ref[...], preferred_element_type=F32)
    for h in range(N_HEADS):
        vh = zv[:, head_cols(h)]
        vo_ref[pl.ds(h, tm, stride=N_HEADS), :] = vh
        if transposed:
            va_ref[head_cols(h), :] = vh.T.astype(BF16)


def _qkv(x, g, w_in, cos_t, sin_t, tm, transposed, layer, depth, earlier):
    n = x.shape[0]
    n_tiles = n // tm
    n_tab = cos_t.shape[0] // tm
    row = lambda i: (i, 0)
    flat = pl.BlockSpec((tm, D_MODEL), row)
    chan = pl.BlockSpec((D_MODEL, tm), lambda i: (0, i))
    if earlier is None:
        assert layer == 0
        cache = pl.BlockSpec((depth, tm * N_HEADS, V_DIM), lambda i: (0, i, 0))
    else:
        cache = pl.BlockSpec((None, tm * N_HEADS, V_DIM), lambda i: (layer, i, 0))
    tab = pl.BlockSpec((tm, LANES), lambda i: (i % n_tab, 0))
    flat_bf16 = jax.ShapeDtypeStruct((n, D_MODEL), BF16)
    chan_bf16 = jax.ShapeDtypeStruct((D_MODEL, n), BF16)
    cache_f32 = jax.ShapeDtypeStruct((depth, n * N_HEADS, V_DIM), F32)
    if transposed:
        out_specs = [chan, flat, chan, cache, cache]
        out_shape = [chan_bf16, flat_bf16, chan_bf16, cache_f32, cache_f32]
    else:
        out_specs = [flat, cache, cache]
        out_shape = [flat_bf16, cache_f32, cache_f32]
    in_specs = [flat, _const_spec((1, D_MODEL)), _in_segment_spec(SEG_Q), _in_segment_spec(SEG_K),
                _in_segment_spec(SEG_V), tab, tab]
    args = [x, g, w_in, w_in, w_in, cos_t, sin_t]
    aliases = {}
    if earlier is not None:
        n_out = len(out_shape)
        aliases = {len(args): n_out - 2, len(args) + 1: n_out - 1}
        in_specs += [pl.BlockSpec(memory_space=pl.ANY)] * 2
        args += list(earlier)
    return pl.pallas_call(
        functools.partial(_qkv_kernel, tm=tm, transposed=transposed, n_aliased=len(aliases)),
        grid=(n_tiles,),
        in_specs=in_specs,
        out_specs=out_specs,
        out_shape=out_shape,
        input_output_aliases=aliases,
        compiler_params=_params(("parallel",)),
        name="qkv",
    )(*args)


def _rnn_kernel(x_ref, g_ref, wxr_ref, wgr_ref, h0_ref, c0_ref, cw_ref, cb_ref,
                wa_ref, ba_ref, wx_ref, bx_ref, lam_ref,
                o_ref, hl_ref, cl_ref,
                xe_ref, gg_ref, al_ref, bl_ref, hin_ref, o32_ref, h_ref, *proj_refs, T, shared):
    t = pl.program_id(1)
    halo = CONV_W - 1
    G = T // SUBLANES
    NB = D_MODEL // LANES
    blocks = [(c, slice(c * LANES, (c + 1) * LANES)) for c in range(NB)]

    @pl.when(t == 0)
    def _():
        for c, cols in blocks:
            xe_ref[c, SUBLANES - halo:SUBLANES, :] = c0_ref[:, cols]
        h_ref[...] = h0_ref[...]

    @pl.when(t > 0)
    def _():
        for c, _ in blocks:
            xe_ref[c, SUBLANES - halo:SUBLANES, :] = xe_ref[c, T + SUBLANES - halo:T + SUBLANES, :]

    def class_rows(ref, c, start):
        return ref[c, pl.ds(start, G, stride=SUBLANES), :]

    def project():
        xn = _rmsnorm_bf16(x_ref[...], g_ref[...])
        return (jnp.dot(xn, wxr_ref[...], preferred_element_type=F32),
                jax.nn.gelu(jnp.dot(xn, wgr_ref[...], preferred_element_type=F32)))

    if shared:
        xr_all_ref, gg_all_ref = proj_refs
        b = pl.program_id(0)

        @pl.when(b == 0)
        def _():
            xr_all_ref[...], gg_all_ref[...] = project()

        r0 = pl.multiple_of(b * T, T)
        xr = xr_all_ref[pl.ds(r0, T), :]
        gg = gg_all_ref[pl.ds(r0, T), :]
    else:
        xr, gg = project()
    for c, cols in blocks:
        xe_ref[c, SUBLANES:SUBLANES + T, :] = xr[:, cols]
        gg_ref[c] = gg[:, cols]

    def class_block(v, k):
        return v[k * G:(k + 1) * G, :]

    for grp in range(RG_GROUPS):
        gcols = slice(grp * MXU_DIM, (grp + 1) * MXU_DIM)
        gblocks = blocks[grp * (MXU_DIM // LANES):(grp + 1) * (MXU_DIM // LANES)]

        classes = []
        for k in range(SUBLANES):
            parts = []
            for c, cols in gblocks:
                acc = cb_ref[:, cols]
                for j in range(CONV_W):
                    acc = acc + class_rows(xe_ref, c, SUBLANES - halo + j + k) * cw_ref[j:j + 1, cols]
                parts.append(acc)
            classes.append(jnp.concatenate(parts, axis=1))
        xc = jnp.concatenate(classes, axis=0)
        xcb = xc.astype(BF16)

        def gate(w_ref, bias_ref):
            z = jnp.dot(xcb, w_ref[grp], preferred_element_type=F32)
            return jax.nn.sigmoid(z + bias_ref[:, gcols])

        r = gate(wa_ref, ba_ref)
        i = gate(wx_ref, bx_ref)
        log_a = r * ((-RG_C) * jax.nn.softplus(-lam_ref[:, gcols]))
        a = jnp.exp(log_a)
        u = jnp.sqrt(1.0 - a * a) * (i * xc)

        a_cum = class_block(a, 0)
        b_cum = class_block(u, 0)
        al_ref[0, :, gcols] = a_cum
        bl_ref[0, :, gcols] = b_cum
        for k in range(1, SUBLANES):
            ak = class_block(a, k)
            b_cum = ak * b_cum + class_block(u, k)
            a_cum = ak * a_cum
            al_ref[k, :, gcols] = a_cum
            bl_ref[k, :, gcols] = b_cum
        h = h_ref[:, gcols]
        for gi in range(G):
            hin_ref[gi:gi + 1, gcols] = h
            h = bl_ref[SUBLANES - 1, gi:gi + 1, gcols] + al_ref[SUBLANES - 1, gi:gi + 1, gcols] * h
        h_ref[:, gcols] = h
        for c, cols in gblocks:
            h_in = hin_ref[:, cols]
            for k in range(SUBLANES):
                hk = bl_ref[k, :, cols] + al_ref[k, :, cols] * h_in
                o32_ref[c, pl.ds(k, G, stride=SUBLANES), :] = hk * class_rows(gg_ref, c, k)
            o_ref[:, cols] = o32_ref[c].astype(o_ref.dtype)
            cl_ref[:, cols] = xe_ref[c, T + SUBLANES - halo:T + SUBLANES, :]
    hl_ref[...] = h_ref[...]


def _rnn(x, g, w_in, h0, c0, cw, cb, wa4, ba, wx4, bx, lam, T):
    B, S, _ = x.shape
    assert S % T == 0 and T % SUBLANES == 0 and T >= CONV_W - 1
    G = T // SUBLANES
    NB = D_MODEL // LANES
    seq = lambda b, t: (b, t, 0)
    per_b = lambda b, t: (b, 0, 0)
    vec = _const_spec((1, D_MODEL))
    wgate = _const_spec((RG_GROUPS, MXU_DIM, MXU_DIM))
    tile = pltpu.VMEM((NB, T, LANES), F32)
    shared = S == T and T < MXU_DIM
    if shared:
        x = x.reshape(B * S, D_MODEL)
        x_spec = _const_spec((B * S, D_MODEL))
        proj_scratch = [pltpu.VMEM((B * S, D_MODEL), F32)] * 2
    else:
        x_spec = pl.BlockSpec((None, T, D_MODEL), seq)
        proj_scratch = []
    return pl.pallas_call(
        functools.partial(_rnn_kernel, T=T, shared=shared),
        grid=(B, S // T),
        in_specs=[
            x_spec,
            vec, _in_segment_spec(SEG_XR), _in_segment_spec(SEG_GR),
            pl.BlockSpec((None, 1, D_MODEL), per_b),
            pl.BlockSpec((None, CONV_W - 1, D_MODEL), per_b),
            _const_spec((CONV_W, D_MODEL)),
            vec, wgate, vec, wgate, vec, vec,
        ],
        out_specs=[
            pl.BlockSpec((None, T, D_MODEL), seq),
            pl.BlockSpec((None, 1, D_MODEL), per_b),
            pl.BlockSpec((None, CONV_W - 1, D_MODEL), per_b),
        ],
        out_shape=[
            jax.ShapeDtypeStruct((B, S, D_MODEL), BF16),
            jax.ShapeDtypeStruct((B, 1, D_MODEL), F32),
            jax.ShapeDtypeStruct((B, CONV_W - 1, D_MODEL), F32),
        ],
        scratch_shapes=[
            pltpu.VMEM((NB, T + SUBLANES, LANES), F32),
            tile,
            pltpu.VMEM((SUBLANES, G, D_MODEL), F32),
            pltpu.VMEM((SUBLANES, G, D_MODEL), F32),
            pltpu.VMEM((G, D_MODEL), F32),
            tile,
            pltpu.VMEM((1, D_MODEL), F32),
        ] + proj_scratch,
        compiler_params=_params(("arbitrary" if shared else "parallel", "arbitrary")),
        name="rnn",
    )(x, g, w_in, w_in, h0, c0, cw, cb, wa4, ba, wx4, bx, lam)


def _attn_p_kernel(qt_ref, k_ref, vt_ref, bias_ref, lq1_ref, lk1_ref, lq2_ref, lk2_ref, g_ref,
                   o_ref, q2_ref, s_ref, mx_ref, p_ref, alpha_ref, m_ref, acc_ref,
                   *, tq, tk, hps, lambda_init):
    qi = pl.program_id(2)
    n_pairs = qi
    heads = range(hps)
    ones = jnp.ones((acc_ref.shape[1] - V_DIM, tk), BF16)
    every = slice(None)
    late = slice(tq, 2 * tq)

    def head_rows(h):
        return slice(h * V_DIM, (h + 1) * V_DIM)

    row = lax.broadcasted_iota(jnp.int32, (V_DIM, tk), 0)
    for h in heads:
        for half in range(2):
            qt = qt_ref[head_rows(h), half * tk:(half + 1) * tk]
            zero = jnp.zeros_like(qt)
            q2_ref[h, :, (2 * half) * tk:(2 * half + 1) * tk] = jnp.where(row < HEAD_DIM, qt, zero)
            q2_ref[h, :, (2 * half + 1) * tk:(2 * half + 2) * tk] = jnp.where(row >= HEAD_DIM, qt, zero)
    m_ref[...] = jnp.full(m_ref.shape, NEG_INF, F32)
    acc_ref[...] = jnp.zeros(acc_ref.shape, F32)
    for h in heads:
        p_ref[h, 1] = jnp.zeros(p_ref.shape[2:], BF16)
        alpha_ref[h, 1] = jnp.ones(alpha_ref.shape[2:], F32)

    def produce(h, kc, slot, diag, cols=every):
        r0 = pl.multiple_of(kc * tk, tk)
        s = jnp.dot(k_ref[pl.ds(r0, tk), head_rows(h)], q2_ref[h, :, cols], preferred_element_type=F32)
        if diag is not None:
            s = s + bias_ref[diag, :, cols]
        s_ref[h, slot, :, cols] = s
        mx_ref[h, slot, :, cols] = jnp.max(s, axis=0, keepdims=True)

    def apply_values(h, kc, slot, cols=every):
        r0 = pl.multiple_of(kc * tk, tk)
        vt1 = jnp.concatenate([vt_ref[head_rows(h), pl.ds(r0, tk)], ones], axis=0)
        pv = jnp.dot(vt1, p_ref[h, slot, :, cols], preferred_element_type=F32)
        acc_ref[h, :, cols] = alpha_ref[h, slot, :, cols] * acc_ref[h, :, cols] + pv

    def softmax(h, slot, cols=every):
        m_old = m_ref[h, :, cols]
        m_new = jnp.maximum(m_old, mx_ref[h, slot, :, cols])
        m_ref[h, :, cols] = m_new
        alpha_ref[h, slot, :, cols] = jnp.exp2(m_old - m_new)
        p_ref[h, slot, :, cols] = jnp.exp2(s_ref[h, slot, :, cols] - m_new).astype(BF16)

    def step(kc, slot, next_diag, next_cols=every, cols=every):
        if next_diag is not False:
            for h in heads:
                produce(h, kc + 1, 1 - slot, next_diag, next_cols)
        for h in heads:
            apply_values(h, jnp.maximum(kc - 1, 0), 1 - slot)
        for h in heads:
            softmax(h, slot, cols)

    def pair(j, then_diag):
        step(2 * j, 0, None)
        step(2 * j + 1, 1, then_diag)

    @pl.when(n_pairs == 0)
    def _():
        for h in heads:
            produce(h, 0, 0, 0)

    @pl.when(n_pairs > 0)
    def _():
        for h in heads:
            produce(h, 0, 0, None)

    def body(j, carry):
        pair(j, None)
        return carry

    lax.fori_loop(0, n_pairs - 1, body, 0)

    @pl.when(n_pairs > 0)
    def _():
        pair(n_pairs - 1, 0)

    step(2 * n_pairs, 0, 1, next_cols=late)
    step(2 * n_pairs + 1, 1, False, cols=late)
    lam = _diff_lambda(lq1_ref, lk1_ref, lq2_ref, lk2_ref, lambda_init)
    for h in heads:
        apply_values(h, 2 * n_pairs + 1, 1, late)
        on = acc_ref[h, :V_DIM] / acc_ref[h, V_DIM:V_DIM + 1]
        ot = jnp.concatenate([on[:, 2 * half * tk:(2 * half + 1) * tk]
                              - lam * on[:, (2 * half + 1) * tk:(2 * half + 2) * tk] for half in range(2)], axis=1)
        o_ref[:, head_rows(h)] = _subln(ot.T, g_ref[...], lambda_init).astype(o_ref.dtype)


def _attn_prompt(qt, k, vt, lq1, lk1, lq2, lk2, g, lambda_init, tq, tk, hps):
    B, S, _ = k.shape
    assert S % tq == 0 and tq == 2 * tk and tq & (tq - 1) == 0 and N_HEADS % hps == 0
    nq = S // tq
    lvec = _const_spec((1, HEAD_DIM))
    slots = lambda shape, dtype: pltpu.VMEM((hps, 2) + shape, dtype)
    key = lax.broadcasted_iota(jnp.int32, (2, tk, 2 * tq), 0) * tk + lax.broadcasted_iota(jnp.int32, (2, tk, 2 * tq), 1)
    col = lax.broadcasted_iota(jnp.int32, (2, tk, 2 * tq), 2)
    qry = (col // tq) * tk + (col & (tk - 1))
    bias = jnp.where(key <= qry, 0.0, NEG_INF).astype(F32)
    ones_rows = 2 * SUBLANES
    return pl.pallas_call(
        functools.partial(_attn_p_kernel, tq=tq, tk=tk, hps=hps, lambda_init=lambda_init),
        grid=(B, N_HEADS // hps, nq),
        in_specs=[
            pl.BlockSpec((hps * V_DIM, tq), lambda b, h, i: (h, b * nq + i)),
            pl.BlockSpec((None, S, hps * V_DIM), lambda b, h, i: (b, 0, h)),
            pl.BlockSpec((hps * V_DIM, S), lambda b, h, i: (h, b)),
            _const_spec((2, tk, 2 * tq)),
            lvec, lvec, lvec, lvec,
            _const_spec((1, V_DIM)),
        ],
        out_specs=pl.BlockSpec((None, tq, hps * V_DIM), lambda b, h, i: (b, i, h)),
        out_shape=jax.ShapeDtypeStruct((B, S, D_MODEL), BF16),
        scratch_shapes=[
            pltpu.VMEM((hps, V_DIM, 2 * tq), BF16),
            slots((tk, 2 * tq), F32),
            slots((1, 2 * tq), F32),
            slots((tk, 2 * tq), BF16),
            slots((1, 2 * tq), F32),
            pltpu.VMEM((hps, 1, 2 * tq), F32),
            pltpu.VMEM((hps, V_DIM + ones_rows, 2 * tq), F32),
        ],
        compiler_params=_params(("parallel", "parallel", "arbitrary")),
        name="attn_prompt",
    )(qt, k, vt, bias, lq1, lk1, lq2, lk2, g)


def _attn_s_kernel(pt_ref, q_ref, kn_ref, vn_ref, lq1_ref, lk1_ref, lq2_ref, lk2_ref, g_ref, *rest,
                   pp, T, lambda_init):
    k_refs = rest[:pp]
    v_refs = rest[pp:2 * pp]
    o_ref, qm_ref, m_ref, l_ref, acc_ref = rest[2 * pp:]
    del pt_ref
    p = pl.program_id(1)
    n_rows = N_HEADS * 2 * T

    def token_major(ref, n_tokens):
        return jnp.concatenate([ref[pl.ds(h, n_tokens, stride=N_HEADS), :] for h in range(N_HEADS)], axis=1)

    @pl.when(p == 0)
    def _():
        q = q_ref[...].astype(F32)
        qrep = jnp.concatenate([q] * (2 * N_HEADS), axis=0)
        rr = lax.broadcasted_iota(jnp.int32, qrep.shape, 0)
        cc = lax.broadcasted_iota(jnp.int32, qrep.shape, 1)
        qm_ref[...] = jnp.where(cc // HEAD_DIM == rr // T, qrep, 0.0).astype(BF16)
        m_ref[...] = jnp.full(m_ref.shape, NEG_INF, F32)
        l_ref[...] = jnp.zeros(l_ref.shape, F32)
        acc_ref[...] = jnp.zeros(acc_ref.shape, F32)

    qm = qm_ref[...]

    def scores(kb):
        return lax.dot_general(qm, kb, _NT, preferred_element_type=F32)

    def fold(s, v_list):
        width = s.shape[1] // len(v_list)
        m_old = m_ref[...]
        m_new = jnp.maximum(m_old, jnp.max(s, axis=-1, keepdims=True))
        alpha = jnp.exp2(m_old - m_new)
        pb = jnp.exp2(s - m_new)
        l_ref[...] = alpha * l_ref[...] + jnp.sum(pb, axis=-1, keepdims=True)
        pb = pb.astype(BF16)
        pv = None
        for i, vb in enumerate(v_list):
            part = jnp.dot(pb[:, i * width:(i + 1) * width], vb, preferred_element_type=F32)
            pv = part if pv is None else pv + part
        acc_ref[...] = alpha * acc_ref[...] + pv
        m_ref[...] = m_new

    s_pages = [scores(token_major(k_refs[i], PAGE_SIZE).astype(BF16)) for i in range(pp)]
    for a in range(0, pp, pp // 2):
        group = range(a, a + pp // 2)
        fold(jnp.concatenate([s_pages[i] for i in group], axis=1),
             [token_major(v_refs[i], PAGE_SIZE).astype(BF16) for i in group])

    @pl.when(p == pl.num_programs(1) - 1)
    def _():
        pad = jnp.zeros((LANES - T, D_MODEL), F32)
        kn = jnp.concatenate([token_major(kn_ref, T), pad], axis=0).astype(BF16)
        vn = jnp.concatenate([token_major(vn_ref, T), pad], axis=0).astype(BF16)
        s = scores(kn)
        qry = lax.broadcasted_iota(jnp.int32, s.shape, 0) % T
        key = lax.broadcasted_iota(jnp.int32, s.shape, 1)
        fold(jnp.where(key <= qry, s, NEG_INF), [vn])

        lam = _diff_lambda(lq1_ref, lk1_ref, lq2_ref, lk2_ref, lambda_init)
        for h in range(N_HEADS):
            r0 = h * 2 * T
            own = slice(h * V_DIM, (h + 1) * V_DIM)
            on = acc_ref[r0:r0 + 2 * T, own] / l_ref[r0:r0 + 2 * T, :]
            dh = on[:T] - lam * on[T:]
            o_ref[:, own] = _subln(dh, g_ref[...], lambda_init).astype(o_ref.dtype)


def _attn_sample(q, k_new, v_new, cache_k, cache_v, layer, page_table, lq1, lk1, lq2, lk2, g,
                 lambda_init, pp):
    DB, T, _ = q.shape
    n_pages = page_table.shape[1]
    assert n_pages % pp == 0 and pp % 2 == 0 and T == SUBLANES and T * N_HEADS <= LANES
    n_rows = N_HEADS * 2 * T
    pt_flat = page_table.reshape(-1)
    seq = pl.BlockSpec((None, T, D_MODEL), lambda b, p, pt: (b, 0, 0))
    new = pl.BlockSpec((None, T * N_HEADS, V_DIM), lambda b, p, pt: (b, 0, 0))
    lvec = pl.BlockSpec((1, HEAD_DIM), lambda b, p, pt: (0, 0))

    def page_spec(i):
        return pl.BlockSpec((None, None, PAGE_SIZE * N_HEADS, V_DIM),
                            lambda b, p, pt: (layer, pt[b * n_pages + p * pp + i], 0, 0))

    grid_spec = pltpu.PrefetchScalarGridSpec(
        num_scalar_prefetch=1,
        grid=(DB, n_pages // pp),
        in_specs=[seq, new, new, lvec, lvec, lvec, lvec,
                  pl.BlockSpec((1, V_DIM), lambda b, p, pt: (0, 0))]
                 + [page_spec(i) for i in range(pp)] * 2,
        out_specs=seq,
        scratch_shapes=[
            pltpu.VMEM((n_rows, D_MODEL), BF16),
            pltpu.VMEM((n_rows, 1), F32),
            pltpu.VMEM((n_rows, 1), F32),
            pltpu.VMEM((n_rows, D_MODEL), F32),
        ],
    )
    return pl.pallas_call(
        functools.partial(_attn_s_kernel, pp=pp, T=T, lambda_init=lambda_init),
        grid_spec=grid_spec,
        out_shape=jax.ShapeDtypeStruct((DB, T, D_MODEL), BF16),
        compiler_params=_params(("parallel", "arbitrary")),
        name="attn_sample",
    )(pt_flat, q, k_new, v_new, lq1, lk1, lq2, lk2, g, *([cache_k] * pp), *([cache_v] * pp))


def _tail_kernel(x_ref, orn_ref, oat_ref, g1_ref, wgr_ref, wga_ref, wbr_ref, wba_ref, wo_ref,
                 g2_ref, wu_ref, wd_ref, gf_ref, o_ref, *, final):
    x = x_ref[...]
    xn = _rmsnorm_bf16(x, g1_ref[...])
    y = jax.nn.sigmoid(jnp.dot(xn, wgr_ref[...], preferred_element_type=F32)) * \
        jnp.dot(orn_ref[...], wbr_ref[...], preferred_element_type=F32)
    y = y + jax.nn.sigmoid(jnp.dot(xn, wga_ref[...], preferred_element_type=F32)) * \
        jnp.dot(oat_ref[...], wba_ref[...], preferred_element_type=F32)
    acc = x + jnp.dot(y.astype(BF16), wo_ref[...], preferred_element_type=F32)
    xn2 = _rmsnorm_bf16(acc, g2_ref[...])
    for c in range(D_FF // D_MODEL):
        cols = slice(c * D_MODEL, (c + 1) * D_MODEL)
        hid = jnp.maximum(jnp.dot(xn2, wu_ref[:, cols], preferred_element_type=F32), 0.0)
        acc = acc + jnp.dot((hid * hid).astype(BF16), wd_ref[cols, :], preferred_element_type=F32)
    if final:
        ms = jnp.mean(acc * acc, axis=-1, keepdims=True)
        acc = acc * lax.rsqrt(ms + NORM_EPS) * gf_ref[...]
    o_ref[...] = acc


def _tail(x, o_rnn, o_attn, g1, w_in, w_branch, wo, g2, wu, wd, gf, final, tm):
    n = x.shape[0]
    row = pl.BlockSpec((tm, D_MODEL), lambda i: (i, 0))
    vec = _const_spec((1, D_MODEL))
    w = _const_spec((D_MODEL, D_MODEL))
    return pl.pallas_call(
        functools.partial(_tail_kernel, final=final),
        grid=(n // tm,),
        in_specs=[row, row, row, vec, _in_segment_spec(SEG_GRNN), _in_segment_spec(SEG_GATTN),
                  _const_spec((D_MODEL, D_MODEL), (0, 0)), _const_spec((D_MODEL, D_MODEL), (1, 0)), w, vec,
                  _const_spec((D_MODEL, D_FF)), _const_spec((D_FF, D_MODEL)), vec],
        out_specs=row,
        out_shape=jax.ShapeDtypeStruct((n, D_MODEL), F32),
        compiler_params=_params(("parallel",)),
        name="tail",
    )(x, o_rnn, o_attn, g1, w_in, w_in, w_branch, w_branch, wo, g2, wu, wd, gf)


def _rope_tables(pos):
    half = HEAD_DIM // 2
    inv = ROPE_THETA ** (-2.0 * jnp.arange(half, dtype=F32) / HEAD_DIM)
    ang = pos.astype(F32)[:, None] * inv[None, :]
    cos = jnp.cos(ang)
    sin = jnp.sin(ang)
    reps = LANES // HEAD_DIM
    return (jnp.concatenate([cos, cos] * reps, axis=1),
            jnp.concatenate([-sin, sin] * reps, axis=1))


def _gate_tiles(w):
    per = MXU_DIM // RG_BLOCK
    w5 = w.reshape(RG_GROUPS, per, RG_BLOCK, RG_BLOCK)
    eye = jnp.eye(per, dtype=w.dtype)
    t = w5[:, :, :, None, :] * eye[None, :, None, :, None]
    return t.reshape(RG_GROUPS, MXU_DIM, MXU_DIM).astype(BF16)


def kernel(x_prompt, x_sample, cache_k, cache_v, state_h, state_conv, page_table, ln1_g, w_in, conv_w, conv_b, rg_wa, rg_ba, rg_wx, rg_bx, rg_lambda, lam_q1, lam_k1, lam_q2, lam_k2, subln_g, w_branch, w_out, ln2_g, w_up, w_down, final_g):
    B, S, _ = x_prompt.shape
    DB, T, _ = x_sample.shape
    depth = w_in.shape[0]
    n_pool = cache_k.shape[1]
    past_len = page_table.shape[1] * PAGE_SIZE

    cos_p, sin_p = _rope_tables(jnp.arange(S))
    cos_s, sin_s = _rope_tables(past_len + jnp.arange(DB * T) % T)
    ck = cache_k.reshape(depth, n_pool, PAGE_SIZE * N_HEADS, V_DIM)
    cv = cache_v.reshape(depth, n_pool, PAGE_SIZE * N_HEADS, V_DIM)

    xp = x_prompt.reshape(B * S, D_MODEL)
    xs = x_sample.reshape(DB * T, D_MODEL)
    h0_p = jnp.zeros((B, 1, D_MODEL), F32)
    c0_p = jnp.zeros((B, CONV_W - 1, D_MODEL), F32)
    gf = final_g.reshape(1, D_MODEL)

    outs = [[] for _ in range(4)]
    kv_p = kv_s = None
    for l in range(depth):
        lambda_init = 0.8 - 0.6 * math.exp(-0.3 * l)
        w_in_bf = w_in[l].astype(BF16)
        w_branch_bf = w_branch[l].astype(BF16)
        wo = w_out[l].astype(BF16)
        wu = w_up[l].astype(BF16)
        wd = w_down[l].astype(BF16)
        wa4 = _gate_tiles(rg_wa[l])
        wx4 = _gate_tiles(rg_wx[l])
        vec = lambda a: a[l].reshape(1, -1)
        lam_vecs = (vec(lam_q1), vec(lam_k1), vec(lam_q2), vec(lam_k2))
        final = l == depth - 1

        def layer(x, n_seq, seq_len, cos_t, sin_t, h0, c0, tm, t_rnn, prompt, kv_all):
            qkv = _qkv(x, vec(ln1_g), w_in_bf, cos_t, sin_t, tm, prompt, l, depth, kv_all)
            kv_all = qkv[-2:]
            o_rnn, h_last, c_last = _rnn(x.reshape(n_seq, seq_len, D_MODEL), vec(ln1_g), w_in_bf, h0, c0,
                                         conv_w[l], vec(conv_b), wa4, vec(rg_ba), wx4, vec(rg_bx),
                                         vec(rg_lambda), t_rnn)
            if prompt:
                qt, ka, vt = qkv[:3]
                o_attn = _attn_prompt(qt, ka.reshape(n_seq, seq_len, D_MODEL), vt, *lam_vecs, vec(subln_g),
                                      lambda_init, ATTN_TQ, ATTN_TK, ATTN_HEADS_PER_STEP)
            else:
                new_rows = lambda a: a[l].reshape(n_seq, seq_len * N_HEADS, V_DIM)
                o_attn = _attn_sample(qkv[0].reshape(n_seq, seq_len, D_MODEL), new_rows(kv_all[0]),
                                      new_rows(kv_all[1]), ck, cv, l, page_table, *lam_vecs, vec(subln_g),
                                      lambda_init, PAGES_PER_STEP)
            x2 = _tail(x, o_rnn.reshape(-1, D_MODEL), o_attn.reshape(-1, D_MODEL), vec(ln1_g),
                       w_in_bf, w_branch_bf, wo, vec(ln2_g), wu, wd, gf, final, tm)
            return x2, kv_all, h_last.reshape(n_seq, D_MODEL), c_last

        xp, kv_p, hp, cp = layer(xp, B, S, cos_p, sin_p, h0_p, c0_p, ROW_TILE, RNN_TILE, True, kv_p)
        xs, kv_s, hs, cs = layer(xs, DB, T, cos_s, sin_s, state_h[l].reshape(DB, 1, D_MODEL),
                                 state_conv[l], DB * T, T, False, kv_s)
        for lst, val in zip(outs, (hp, cp, hs, cs)):
            lst.append(val)

    h_p, c_p, h_s, c_s = (jnp.stack(o) for o in outs)
    as_cache = lambda a, n_seq, seq_len: a.reshape(depth, n_seq, seq_len, N_HEADS, V_DIM)
    return (xp.reshape(B, S, D_MODEL), xs.reshape(DB, T, D_MODEL),
            as_cache(kv_p[0], B, S), as_cache(kv_p[1], B, S), h_p, c_p,
            as_cache(kv_s[0], DB, T), as_cache(kv_s[1], DB, T), h_s, c_s)
```
